```python
import jax, jax.numpy as jnp
from jax import lax
import numpy as np

D_MODEL = 1024
BATCH = 8
SEQ = 4096
DEPTH = 2

D_MIX = D_MODEL
CONV_W = D_MIX // 4
POOL_W = D_MIX // 4
ATTN_W = D_MIX - CONV_W - POOL_W
HEAD_DIM = 64
N_HEADS = ATTN_W // HEAD_DIM
CONV_K = 31
POOL_WINDOWS = (2, 4, 8, 16)
N_POOL_GROUPS = len(POOL_WINDOWS)
POOL_GROUP = POOL_W // N_POOL_GROUPS
GRID_W = 64
WIN_R_MAX = 8
WIN_C = 16
Q_COLS = WIN_C
K_COLS = 2 * WIN_C
D_FF = ((8 * D_MODEL // 3 + 127) // 128) * 128
IN_W = 2 * CONV_W + POOL_W + 3 * ATTN_W
EPS = 1e-6
NEG = -1e30

kernel_name = 'hybrid_conv_pool_natten_macaron_encoder'


def rmsnorm(x, g):
    x32 = x.astype(jnp.float32)
    y = x32 * lax.rsqrt(jnp.mean(x32 * x32, axis=-1, keepdims=True) + EPS)
    return (y * g.astype(jnp.float32)).astype(x.dtype)


def layernorm(x, g, b):
    x32 = x.astype(jnp.float32)
    mu = jnp.mean(x32, axis=-1, keepdims=True)
    var = jnp.mean(jnp.square(x32 - mu), axis=-1, keepdims=True)
    y = (x32 - mu) * lax.rsqrt(var + EPS)
    return (y * g.astype(jnp.float32) + b.astype(jnp.float32)).astype(x.dtype)


def swiglu(h, w_gate, w_up, w_down):
    return (jax.nn.silu(h @ w_gate) * (h @ w_up)) @ w_down


def conv_module(a, gate, dw, dw_b, ln_g, ln_b, pw):
    u = a * jax.nn.sigmoid(gate)
    u = lax.conv_general_dilated(
        u, dw[:, None, :], window_strides=(1,),
        padding=[(CONV_K // 2, CONV_K // 2)],
        dimension_numbers=('NWC', 'WIO', 'NWC'),
        feature_group_count=u.shape[-1]) + dw_b
    u = jax.nn.silu(layernorm(u, ln_g, ln_b))
    return u @ pw


def pool_mixer(p, w_group, scale):
    b, s, _ = p.shape
    pg = p.reshape(b, s, N_POOL_GROUPS, POOL_GROUP)
    csum = jnp.concatenate(
        [jnp.zeros((b, 1, N_POOL_GROUPS, POOL_GROUP), jnp.float32),
         jnp.cumsum(pg.astype(jnp.float32), axis=1)], axis=1)
    t = jnp.arange(s)
    means = []
    for gi, w in enumerate(POOL_WINDOWS):
        lo = jnp.clip(t - w // 2, 0, s)
        hi = jnp.clip(t - w // 2 + w, 0, s)
        win_sum = csum[:, hi, gi] - csum[:, lo, gi]
        means.append(win_sum / (hi - lo).astype(jnp.float32)[:, None])
    pooled = jnp.stack(means, axis=2)
    mixed = (pooled - pg.astype(jnp.float32)).astype(p.dtype)
    y = jnp.einsum('bsgc,gcd->bsgd', mixed, w_group).reshape(b, s, POOL_W)
    return y * scale


def _column_blocks():
    n_cb = GRID_W // Q_COLS
    qcol = np.arange(GRID_W).reshape(n_cb, Q_COLS)
    kstart = np.clip(np.arange(n_cb) * Q_COLS - WIN_C // 2, 0, GRID_W - K_COLS)
    kcol = kstart[:, None] + np.arange(K_COLS)
    c0 = np.clip(qcol - WIN_C // 2, 0, GRID_W - WIN_C)
    valid = (kcol[:, None, :] >= c0[:, :, None]) & (kcol[:, None, :] < c0[:, :, None] + WIN_C)
    dc_idx = np.clip(kcol[:, None, :] - qcol[:, :, None] + WIN_C - 1, 0, 2 * WIN_C - 2)
    return kcol, valid, dc_idx


def neighbourhood_attention(q, k, v, rpb):
    b, s, h, d = q.shape
    rows = s // GRID_W
    kr = min(WIN_R_MAX, rows)
    kcol, valid, dc_idx = _column_blocks()
    n_cb = kcol.shape[0]
    qg = q.reshape(b, rows, n_cb, Q_COLS, h, d).transpose(1, 0, 4, 2, 3, 5)
    kg = k.reshape(b, rows, GRID_W, h, d).transpose(0, 3, 1, 2, 4)
    vg = v.reshape(b, rows, GRID_W, h, d).transpose(0, 3, 1, 2, 4)
    scale = HEAD_DIM ** -0.5
    mask = jnp.asarray(valid)[:, :, None, :]

    def one_row(args):
        q_r, r = args
        r0 = jnp.clip(r - kr // 2, 0, rows - kr)
        k_blk = lax.dynamic_slice_in_dim(kg, r0, kr, axis=2)[:, :, :, kcol]
        v_blk = lax.dynamic_slice_in_dim(vg, r0, kr, axis=2)[:, :, :, kcol]
        sc = jnp.einsum('bhnqd,bhinjd->bhnqij', q_r.astype(jnp.float32),
                        k_blk.astype(jnp.float32)) * scale
        dr_idx = r0 + jnp.arange(kr) - r + WIN_R_MAX - 1
        bias = jnp.take(rpb, dr_idx, axis=1)[:, :, dc_idx]
        bias = bias.transpose(0, 2, 3, 1, 4).astype(jnp.float32)
        sc = jnp.where(mask, sc + bias, NEG)
        p = jax.nn.softmax(sc.reshape(b, h, n_cb, Q_COLS, kr * K_COLS), axis=-1).reshape(sc.shape)
        return jnp.einsum('bhnqij,bhinjd->bhnqd', p.astype(v_blk.dtype), v_blk)

    out = lax.map(one_row, (qg, jnp.arange(rows)))
    return out.transpose(1, 0, 3, 4, 2, 5).reshape(b, s, h * d)


def setup_inputs(seed: int = 0) -> dict:
    key = jax.random.key(seed)
    ks = jax.random.split(key, 24)
    f32 = jnp.float32

    def nrm(k, shape, scale):
        return jax.random.normal(k, shape, f32) * scale

    def gain(k, shape):
        return 1.0 + 0.05 * jax.random.normal(k, shape, f32)

    L = DEPTH
    return {
        'x': jax.random.normal(ks[0], (BATCH, SEQ, D_MODEL), f32),
        'ffn1_norm': gain(ks[1], (L, D_MODEL)),
        'ffn1_gate': nrm(ks[2], (L, D_MODEL, D_FF), D_MODEL ** -0.5),
        'ffn1_up': nrm(ks[3], (L, D_MODEL, D_FF), D_MODEL ** -0.5),
        'ffn1_down': nrm(ks[4], (L, D_FF, D_MODEL), D_FF ** -0.5),
        'mix_norm': gain(ks[5], (L, D_MODEL)),
        'w_in': nrm(ks[6], (L, D_MODEL, IN_W), D_MODEL ** -0.5),
        'conv_dw': nrm(ks[7], (L, CONV_K, CONV_W), CONV_K ** -0.5),
        'conv_dw_b': nrm(ks[8], (L, CONV_W), 0.02),
        'conv_ln_g': gain(ks[9], (L, CONV_W)),
        'conv_ln_b': nrm(ks[10], (L, CONV_W), 0.02),
        'conv_pw': nrm(ks[11], (L, CONV_W, CONV_W), CONV_W ** -0.5),
        'pool_w': nrm(ks[12], (L, N_POOL_GROUPS, POOL_GROUP, POOL_GROUP), POOL_GROUP ** -0.5),
        'pool_scale': gain(ks[13], (L, POOL_W)),
        'q_norm': gain(ks[14], (L, HEAD_DIM)),
        'k_norm': gain(ks[15], (L, HEAD_DIM)),
        'rpb': nrm(ks[16], (L, N_HEADS, 2 * WIN_R_MAX - 1, 2 * WIN_C - 1), 0.1),
        'w_out': nrm(ks[17], (L, D_MIX, D_MODEL), D_MIX ** -0.5),
        'ffn2_norm': gain(ks[18], (L, D_MODEL)),
        'ffn2_gate': nrm(ks[19], (L, D_MODEL, D_FF), D_MODEL ** -0.5),
        'ffn2_up': nrm(ks[20], (L, D_MODEL, D_FF), D_MODEL ** -0.5),
        'ffn2_down': nrm(ks[21], (L, D_FF, D_MODEL), D_FF ** -0.5),
    }


def reference(x, ffn1_norm, ffn1_gate, ffn1_up, ffn1_down, mix_norm, w_in,
              conv_dw, conv_dw_b, conv_ln_g, conv_ln_b, conv_pw, pool_w, pool_scale,
              q_norm, k_norm, rpb, w_out, ffn2_norm, ffn2_gate, ffn2_up, ffn2_down):
    b, s, _ = x.shape
    splits = np.cumsum([CONV_W, CONV_W, POOL_W, ATTN_W, ATTN_W])
    for l in range(DEPTH):
        x = x + 0.5 * swiglu(rmsnorm(x, ffn1_norm[l]), ffn1_gate[l], ffn1_up[l], ffn1_down[l])
        h = rmsnorm(x, mix_norm[l])
        u = h @ w_in[l]
        c_a, c_g, p_in, q, k, v = jnp.split(u, splits, axis=-1)
        c_out = conv_module(c_a, c_g, conv_dw[l], conv_dw_b[l], conv_ln_g[l], conv_ln_b[l], conv_pw[l])
        p_out = pool_mixer(p_in, pool_w[l], pool_scale[l])
        q = rmsnorm(q.reshape(b, s, N_HEADS, HEAD_DIM), q_norm[l])
        k = rmsnorm(k.reshape(b, s, N_HEADS, HEAD_DIM), k_norm[l])
        v = v.reshape(b, s, N_HEADS, HEAD_DIM)
        a_out = neighbourhood_attention(q, k, v, rpb[l])
        x = x + jnp.concatenate([c_out, p_out, a_out], axis=-1) @ w_out[l]
        x = x + 0.5 * swiglu(rmsnorm(x, ffn2_norm[l]), ffn2_gate[l], ffn2_up[l], ffn2_down[l])
    return x
```

```python
import functools

import numpy as np
import jax
import jax.numpy as jnp
from jax import lax
from jax.experimental import pallas as pl
from jax.experimental.pallas import tpu as pltpu

D_MODEL = 1024
DEPTH = 2
CONV_W = 256
POOL_W = 256
ATTN_W = 512
HEAD_DIM = 64
N_HEADS = 8
CONV_K = 31
POOL_WINDOWS = (2, 4, 8, 16)
POOL_GROUP = 64
GRID_W = 64
WIN_R = 8
WIN_C = 16
D_FF = 2816
IN_W = 2 * CONV_W + POOL_W + 3 * ATTN_W
MIX_IN_W = 2 * CONV_W + POOL_W
EPS = 1e-6
NEG = -1e30

F32 = jnp.float32
BF16 = jnp.bfloat16

VMEM_LIMIT_BYTES = 56 * 1024 * 1024

FFN_TOKENS = 512
FF_CHUNK = 1408
PROJ_TOKENS = 512
MIX_ROWS = 8
MIX_TOKENS = MIX_ROWS * GRID_W
HALO = 16
ROW_CHUNK = 64
HEAD_GROUP = 4
GROUP_W = HEAD_GROUP * HEAD_DIM
N_GROUPS = N_HEADS // HEAD_GROUP
N_DR = 2 * WIN_R - 1
N_DR_PAIRS = N_DR - 1


def _const_spec(shape):
    nd = len(shape)
    return pl.BlockSpec(shape, lambda *_: (0,) * nd, pipeline_mode=pl.Buffered(1))


def _rms(x, g):
    return x * lax.rsqrt(jnp.mean(x * x, axis=-1, keepdims=True) + EPS) * g


def _ffn_body(x_ref, g_ref, wg_ref, wu_ref, wd_ref, o_ref):
    x = x_ref[...]
    h = _rms(x, g_ref[...]).astype(BF16)
    y = None
    for c in range(D_FF // FF_CHUNK):
        sl = slice(c * FF_CHUNK, (c + 1) * FF_CHUNK)
        gate = jnp.dot(h, wg_ref[:, sl], preferred_element_type=F32)
        up = jnp.dot(h, wu_ref[:, sl], preferred_element_type=F32)
        act = (gate * jax.nn.sigmoid(gate) * up).astype(BF16)
        part = jnp.dot(act, wd_ref[sl, :], preferred_element_type=F32)
        y = part if y is None else y + part
    o_ref[...] = x + 0.5 * y


def _ffn(x2d, g, wg, wu, wd):
    t = x2d.shape[0]
    return pl.pallas_call(
        _ffn_body,
        grid=(t // FFN_TOKENS,),
        in_specs=[
            pl.BlockSpec((FFN_TOKENS, D_MODEL), lambda i: (i, 0)),
            _const_spec((1, D_MODEL)),
            _const_spec((D_MODEL, D_FF)),
            _const_spec((D_MODEL, D_FF)),
            _const_spec((D_FF, D_MODEL)),
        ],
        out_specs=pl.BlockSpec((FFN_TOKENS, D_MODEL), lambda i: (i, 0)),
        out_shape=jax.ShapeDtypeStruct((t, D_MODEL), F32),
        compiler_params=pltpu.CompilerParams(
            dimension_semantics=("arbitrary",), vmem_limit_bytes=VMEM_LIMIT_BYTES),
        name="ffn",
    )(x2d, g, wg, wu, wd)


def _head_norm(x, gain, ones_bd):
    sq = x * x
    hi = sq.astype(BF16)
    lo = (sq - hi.astype(F32)).astype(BF16)
    ss = jnp.dot(jnp.concatenate([hi, lo], axis=1), ones_bd, preferred_element_type=F32)
    return x * lax.rsqrt(ss * (1.0 / HEAD_DIM) + EPS) * gain


def _proj_body(x_ref, g_ref, w_ref, qg_ref, kg_ref, ones_ref, mix_ref, q_ref, k_ref, v_ref):
    h = _rms(x_ref[...], g_ref[...]).astype(BF16)
    u = jnp.dot(h, w_ref[...], preferred_element_type=F32)
    mix_ref[...] = u[:, :MIX_IN_W]
    q = u[:, MIX_IN_W:MIX_IN_W + ATTN_W]
    k = u[:, MIX_IN_W + ATTN_W:MIX_IN_W + 2 * ATTN_W]
    v = u[:, MIX_IN_W + 2 * ATTN_W:]
    ones_bd = ones_ref[...]
    q_ref[...] = (_head_norm(q, qg_ref[...], ones_bd) * (HEAD_DIM ** -0.5)).astype(BF16)
    k_ref[...] = _head_norm(k, kg_ref[...], ones_bd).astype(BF16)
    v_ref[...] = v.astype(BF16)


def _proj(x2d, g, w_in, q_gain, k_gain, ones_bd):
    t = x2d.shape[0]
    tok = lambda w: pl.BlockSpec((PROJ_TOKENS, w), lambda i: (i, 0))
    return pl.pallas_call(
        _proj_body,
        grid=(t // PROJ_TOKENS,),
        in_specs=[
            tok(D_MODEL),
            _const_spec((1, D_MODEL)),
            _const_spec((D_MODEL, IN_W)),
            _const_spec((1, ATTN_W)),
            _const_spec((1, ATTN_W)),
            _const_spec((2 * ATTN_W, ATTN_W)),
        ],
        out_specs=[tok(MIX_IN_W), tok(ATTN_W), tok(ATTN_W), tok(ATTN_W)],
        out_shape=[
            jax.ShapeDtypeStruct((t, MIX_IN_W), F32),
            jax.ShapeDtypeStruct((t, ATTN_W), BF16),
            jax.ShapeDtypeStruct((t, ATTN_W), BF16),
            jax.ShapeDtypeStruct((t, ATTN_W), BF16),
        ],
        compiler_params=pltpu.CompilerParams(
            dimension_semantics=("arbitrary",), vmem_limit_bytes=VMEM_LIMIT_BYTES),
        name="proj_in",
    )(x2d, g, w_in, q_gain, k_gain, ones_bd)


def _bias_table_body(rpb_ref, o_ref):
    g = pl.program_id(0)
    j = pl.program_id(1)
    qc = lax.broadcasted_iota(jnp.int32, (GRID_W, 2 * GRID_W), 0)
    lane = lax.broadcasted_iota(jnp.int32, (GRID_W, 2 * GRID_W), 1)
    kc = lane % GRID_W
    c0 = jnp.clip(qc - WIN_C // 2, 0, GRID_W - WIN_C)
    valid = (kc >= c0) & (kc < c0 + WIN_C)
    dc = kc - qc + (WIN_C - 1)
    second = lane >= GRID_W
    blocks = []
    for hh in range(HEAD_GROUP):
        head = g * HEAD_GROUP + hh
        t = jnp.zeros((GRID_W, 2 * GRID_W), F32)
        for c in range(2 * WIN_C - 1):
            val = jnp.where(second, rpb_ref[head, j + 1, c], rpb_ref[head, j, c])
            t = jnp.where(dc == c, val, t)
        blocks.append(jnp.where(valid, t, NEG))
    o_ref[0, 0] = jnp.concatenate(blocks, axis=0)


def _bias_table(rpb_l):
    return pl.pallas_call(
        _bias_table_body,
        grid=(N_GROUPS, N_DR_PAIRS),
        in_specs=[pl.BlockSpec(memory_space=pltpu.SMEM)],
        out_specs=pl.BlockSpec((1, 1, GROUP_W, 2 * GRID_W), lambda g, j: (g, j, 0, 0)),
        out_shape=jax.ShapeDtypeStruct((N_GROUPS, N_DR_PAIRS, GROUP_W, 2 * GRID_W), F32),
        compiler_params=pltpu.CompilerParams(dimension_semantics=("arbitrary", "arbitrary")),
        name="bias_table",
    )(rpb_l)


def _mix_body(x_ref, main_ref, left_ref, right_ref, q_ref, k_ref, v_ref, tab_ref,
              dw_ref, dwb_ref, lng_ref, lnb_ref, pw_ref, poolw_ref, pools_ref, wout_ref,
              o_ref, conv_ext, pool_ext, cat_ref):
    i = pl.program_id(1)
    n_blocks = pl.num_programs(1)
    seq = n_blocks * MIX_TOKENS

    def glu(blk):
        return blk[:, :CONV_W] * jax.nn.sigmoid(blk[:, CONV_W:2 * CONV_W])

    left = left_ref[0]
    right = right_ref[0]
    lmask = jnp.where(i > 0, 1.0, 0.0).astype(F32)
    rmask = jnp.where(i < n_blocks - 1, 1.0, 0.0).astype(F32)
    main = main_ref[0]
    conv_ext[0:HALO, :] = glu(left) * lmask
    conv_ext[HALO:HALO + MIX_TOKENS, :] = glu(main)
    conv_ext[HALO + MIX_TOKENS:, :] = glu(right) * rmask
    pool_ext[0:HALO, :] = left[:, 2 * CONV_W:] * lmask
    pool_ext[HALO:HALO + MIX_TOKENS, :] = main[:, 2 * CONV_W:]
    pool_ext[HALO + MIX_TOKENS:, :] = right[:, 2 * CONV_W:] * rmask

    lane128 = lax.broadcasted_iota(jnp.int32, (ROW_CHUNK, 128), 1)
    first_half = lane128 < POOL_GROUP
    dwb = dwb_ref[...]
    lng = lng_ref[...]
    lnb = lnb_ref[...]
    pools = pools_ref[...]
    pw = pw_ref[...]
    poolw = poolw_ref[...]

    for c in range(MIX_TOKENS // ROW_CHUNK):
        base = c * ROW_CHUNK
        acc = jnp.zeros((ROW_CHUNK, CONV_W), F32)
        for j in range(CONV_K):
            off = HALO + base + j - CONV_K // 2
            acc = acc + conv_ext[off:off + ROW_CHUNK, :] * dw_ref[j:j + 1, :]
        y = acc + dwb
        mu = jnp.mean(y, axis=-1, keepdims=True)
        yc = y - mu
        var = jnp.mean(yc * yc, axis=-1, keepdims=True)
        yn = yc * lax.rsqrt(var + EPS) * lng + lnb
        s = (yn * jax.nn.sigmoid(yn)).astype(BF16)
        c_out = jnp.dot(s, pw, preferred_element_type=F32)
        cat_ref[base:base + ROW_CHUNK, 0:CONV_W] = c_out.astype(BF16)

        def psum(lo, hi, col):
            tot = None
            for d in range(lo, hi):
                off = HALO + base + d
                piece = pool_ext[off:off + ROW_CHUNK, col * 128:(col + 1) * 128]
                tot = piece if tot is None else tot + piece
            return tot

        tpos = i * MIX_TOKENS + base + lax.broadcasted_iota(jnp.int32, (ROW_CHUNK, 128), 0)

        def count(w):
            lo = jnp.maximum(tpos - w // 2, 0)
            hi = jnp.minimum(tpos - w // 2 + w, seq)
            return (hi - lo).astype(F32)

        a2 = psum(-1, 1, 0)
        a4 = a2 + psum(-2, -1, 0) + psum(1, 2, 0)
        a8 = psum(-4, 4, 1)
        a16 = a8 + psum(-8, -4, 1) + psum(4, 8, 1)
        mean01 = jnp.where(first_half, a2 / count(2), a4 / count(4))
        mean23 = jnp.where(first_half, a8 / count(8), a16 / count(16))
        tok = pool_ext[HALO + base:HALO + base + ROW_CHUNK, :]
        mixed = (jnp.concatenate([mean01, mean23], axis=1) - tok).astype(BF16)
        p_out = jnp.dot(mixed, poolw, preferred_element_type=F32) * pools
        cat_ref[base:base + ROW_CHUNK, CONV_W:CONV_W + POOL_W] = p_out.astype(BF16)

    n_rows = n_blocks * MIX_ROWS
    lane_head = lax.broadcasted_iota(jnp.int32, (GRID_W, GROUP_W), 1) // HEAD_DIM

    def row_step(r, carry):
        row = i * MIX_ROWS + r
        r0 = jnp.clip(row - WIN_R // 2, 0, n_rows - WIN_R)
        j0 = r0 - row + (WIN_R - 1)
        kstart = pl.multiple_of(r0 * GRID_W, GRID_W)
        qstart = pl.multiple_of(r * GRID_W, GRID_W)
        for g in range(N_GROUPS):
            cols = slice(g * GROUP_W, (g + 1) * GROUP_W)
            q_r = q_ref[0, pl.ds(qstart, GRID_W), cols]
            kwin = k_ref[0, pl.ds(kstart, WIN_R * GRID_W), cols]
            vwin = v_ref[0, pl.ds(kstart, WIN_R * GRID_W), cols]
            zero = jnp.zeros_like(q_r)
            q_bd = jnp.concatenate(
                [jnp.where(lane_head == hh, q_r, zero) for hh in range(HEAD_GROUP)], axis=0)
            sc = lax.dot_general(q_bd, kwin, (((1,), (1,)), ((), ())),
                                 preferred_element_type=F32)
            bias = jnp.concatenate(
                [tab_ref[g, j0 + 2 * m] for m in range(WIN_R // 2)], axis=1)
            sc = sc + bias
            mx = jnp.max(sc, axis=-1, keepdims=True)
            p = jnp.exp(sc - mx)
            den = jnp.sum(p, axis=-1, keepdims=True)
            o_all = jnp.dot(p.astype(BF16), vwin, preferred_element_type=F32) / den
            out = jnp.zeros((GRID_W, GROUP_W), F32)
            for hh in range(HEAD_GROUP):
                out = jnp.where(lane_head == hh, o_all[hh * GRID_W:(hh + 1) * GRID_W, :], out)
            cat_ref[pl.ds(qstart, GRID_W),
                    CONV_W + POOL_W + g * GROUP_W:CONV_W + POOL_W + (g + 1) * GROUP_W] = (
                        out.astype(BF16))
        return carry

    lax.fori_loop(0, MIX_ROWS, row_step, 0)

    o_ref[0] = x_ref[0] + jnp.dot(cat_ref[...], wout_ref[...], preferred_element_type=F32)


def _mix(x3d, mix3d, q3d, k3d, v3d, table, dw, dwb, lng, lnb, pw, poolw, pools, wout):
    b, s, _ = x3d.shape
    n_blocks = s // MIX_TOKENS
    per_blk = MIX_TOKENS // HALO
    n_halo = s // HALO
    blk = lambda w: pl.BlockSpec((1, MIX_TOKENS, w), lambda bi, i: (bi, i, 0))
    full = lambda w: pl.BlockSpec((1, s, w), lambda bi, i: (bi, 0, 0))
    return pl.pallas_call(
        _mix_body,
        grid=(b, n_blocks),
        in_specs=[
            blk(D_MODEL),
            blk(MIX_IN_W),
            pl.BlockSpec((1, HALO, MIX_IN_W),
                         lambda bi, i: (bi, jnp.maximum(i * per_blk - 1, 0), 0)),
            pl.BlockSpec((1, HALO, MIX_IN_W),
                         lambda bi, i: (bi, jnp.minimum((i + 1) * per_blk, n_halo - 1), 0)),
            blk(ATTN_W),
            full(ATTN_W),
            full(ATTN_W),
            _const_spec(table.shape),
            _const_spec(dw.shape),
            _const_spec(dwb.shape),
            _const_spec(lng.shape),
            _const_spec(lnb.shape),
            _const_spec(pw.shape),
            _const_spec(poolw.shape),
            _const_spec(pools.shape),
            _const_spec(wout.shape),
        ],
        out_specs=blk(D_MODEL),
        out_shape=jax.ShapeDtypeStruct((b, s, D_MODEL), F32),
        scratch_shapes=[
            pltpu.VMEM((MIX_TOKENS + 2 * HALO, CONV_W), F32),
            pltpu.VMEM((MIX_TOKENS + 2 * HALO, POOL_W), F32),
            pltpu.VMEM((MIX_TOKENS, D_MODEL), BF16),
        ],
        compiler_params=pltpu.CompilerParams(
            dimension_semantics=("arbitrary", "arbitrary"), vmem_limit_bytes=VMEM_LIMIT_BYTES),
        name="mixer",
    )(x3d, mix3d, mix3d, mix3d, q3d, k3d, v3d, table, dw, dwb, lng, lnb, pw, poolw, pools, wout)


def _block_diag(blocks):
    n, r, c = blocks.shape
    eye = jnp.eye(n, dtype=blocks.dtype)
    return (eye[:, None, :, None] * blocks[:, :, None, :]).reshape(n * r, n * c)


def kernel(x, ffn1_norm, ffn1_gate, ffn1_up, ffn1_down, mix_norm, w_in, conv_dw, conv_dw_b,
           conv_ln_g, conv_ln_b, conv_pw, pool_w, pool_scale, q_norm, k_norm, rpb, w_out,
           ffn2_norm, ffn2_gate, ffn2_up, ffn2_down):
    b, s, d = x.shape
    assert d == D_MODEL and s % MIX_TOKENS == 0 and (b * s) % FFN_TOKENS == 0
    t = b * s
    row = lambda a: a.reshape(1, -1)
    head_ones = np.kron(np.eye(N_HEADS, dtype=np.float32),
                        np.ones((HEAD_DIM, HEAD_DIM), np.float32))
    ones_bd = jnp.asarray(np.concatenate([head_ones, head_ones], axis=0), dtype=BF16)

    x2d = x.reshape(t, d)
    for l in range(DEPTH):
        x2d = _ffn(x2d, row(ffn1_norm[l]), ffn1_gate[l].astype(BF16),
                   ffn1_up[l].astype(BF16), ffn1_down[l].astype(BF16))
        mix_in, q, k, v = _proj(
            x2d, row(mix_norm[l]), w_in[l].astype(BF16),
            row(jnp.tile(q_norm[l], N_HEADS)), row(jnp.tile(k_norm[l], N_HEADS)), ones_bd)
        table = _bias_table(rpb[l])
        x3d = _mix(
            x2d.reshape(b, s, d), mix_in.reshape(b, s, MIX_IN_W),
            q.reshape(b, s, ATTN_W), k.reshape(b, s, ATTN_W), v.reshape(b, s, ATTN_W),
            table, conv_dw[l], row(conv_dw_b[l]), row(conv_ln_g[l]), row(conv_ln_b[l]),
            conv_pw[l].astype(BF16), _block_diag(pool_w[l]).astype(BF16),
            row(pool_scale[l]), w_out[l].astype(BF16))
        x2d = _ffn(x3d.reshape(t, d), row(ffn2_norm[l]), ffn2_gate[l].astype(BF16),
                   ffn2_up[l].astype(BF16), ffn2_down[l].astype(BF16))
    return x2d.reshape(b, s, d)
```

```python
import functools

import numpy as np
import jax
import jax.numpy as jnp
from jax import lax
from jax.experimental import pallas as pl
from jax.experimental.pallas import tpu as pltpu

D_MODEL = 1024
DEPTH = 2
CONV_W = 256
POOL_W = 256
ATTN_W = 512
HEAD_DIM = 64
N_HEADS = 8
CONV_K = 31
POOL_WINDOWS = (2, 4, 8, 16)
POOL_GROUP = 64
GRID_W = 64
WIN_R = 8
WIN_C = 16
D_FF = 2816
IN_W = 2 * CONV_W + POOL_W + 3 * ATTN_W
MIX_IN_W = 2 * CONV_W + POOL_W
EPS = 1e-6
NEG = -1e30

F32 = jnp.float32
BF16 = jnp.bfloat16

VMEM_LIMIT_BYTES = 56 * 1024 * 1024

FFN_TOKENS = 512
MXU_TILE = 256
FF_CHUNKS = (6 * MXU_TILE, 5 * MXU_TILE)
PROJ_TOKENS = 512
MIX_ROWS = 8
MIX_TOKENS = MIX_ROWS * GRID_W
HALO = 16
ROW_CHUNK = 64
HEAD_GROUP = 4
GROUP_W = HEAD_GROUP * HEAD_DIM
N_GROUPS = N_HEADS // HEAD_GROUP
N_DR = 2 * WIN_R - 1
N_DR_PAIRS = N_DR - 1


def _const_spec(shape):
    nd = len(shape)
    return pl.BlockSpec(shape, lambda *_: (0,) * nd, pipeline_mode=pl.Buffered(1))


def _rms(x, g):
    return x * lax.rsqrt(jnp.mean(x * x, axis=-1, keepdims=True) + EPS) * g


def _ffn_body(x_ref, g_ref, wg_ref, wu_ref, wd_ref, o_ref):
    x = x_ref[...]
    h = _rms(x, g_ref[...]).astype(BF16)
    y = None
    start = 0
    for width in FF_CHUNKS:
        sl = slice(start, start + width)
        start += width
        gate = jnp.dot(h, wg_ref[:, sl], preferred_element_type=F32)
        up = jnp.dot(h, wu_ref[:, sl], preferred_element_type=F32)
        act = (gate * jax.nn.sigmoid(gate) * up).astype(BF16)
        part = jnp.dot(act, wd_ref[sl, :], preferred_element_type=F32)
        y = part if y is None else y + part
    o_ref[...] = x + 0.5 * y


def _ffn(x2d, g, wg, wu, wd):
    t = x2d.shape[0]
    return pl.pallas_call(
        _ffn_body,
        grid=(t // FFN_TOKENS,),
        in_specs=[
            pl.BlockSpec((FFN_TOKENS, D_MODEL), lambda i: (i, 0)),
            _const_spec((1, D_MODEL)),
            _const_spec((D_MODEL, D_FF)),
            _const_spec((D_MODEL, D_FF)),
            _const_spec((D_FF, D_MODEL)),
        ],
        out_specs=pl.BlockSpec((FFN_TOKENS, D_MODEL), lambda i: (i, 0)),
        out_shape=jax.ShapeDtypeStruct((t, D_MODEL), F32),
        compiler_params=pltpu.CompilerParams(
            dimension_semantics=("arbitrary",), vmem_limit_bytes=VMEM_LIMIT_BYTES),
        name="ffn",
    )(x2d, g, wg, wu, wd)


def _head_norm(x, gain, ones_bd):
    sq = x * x
    hi = sq.astype(BF16)
    lo = (sq - hi.astype(F32)).astype(BF16)
    ss = jnp.dot(jnp.concatenate([hi, lo], axis=1), ones_bd, preferred_element_type=F32)
    return x * lax.rsqrt(ss * (1.0 / HEAD_DIM) + EPS) * gain


def _proj_body(x_ref, g_ref, w_ref, qg_ref, kg_ref, ones_ref, mix_ref, q_ref, k_ref, v_ref):
    h = _rms(x_ref[...], g_ref[...]).astype(BF16)
    u = jnp.dot(h, w_ref[...], preferred_element_type=F32)
    mix_ref[...] = u[:, :MIX_IN_W]
    q = u[:, MIX_IN_W:MIX_IN_W + ATTN_W]
    k = u[:, MIX_IN_W + ATTN_W:MIX_IN_W + 2 * ATTN_W]
    v = u[:, MIX_IN_W + 2 * ATTN_W:]
    ones_bd = ones_ref[...]
    q_ref[...] = (_head_norm(q, qg_ref[...], ones_bd) * (HEAD_DIM ** -0.5)).astype(BF16)
    k_ref[...] = _head_norm(k, kg_ref[...], ones_bd).astype(BF16)
    v_ref[...] = v.astype(BF16)


def _proj(x2d, g, w_in, q_gain, k_gain, ones_bd):
    t = x2d.shape[0]
    tok = lambda w: pl.BlockSpec((PROJ_TOKENS, w), lambda i: (i, 0))
    return pl.pallas_call(
        _proj_body,
        grid=(t // PROJ_TOKENS,),
        in_specs=[
            tok(D_MODEL),
            _const_spec((1, D_MODEL)),
            _const_spec((D_MODEL, IN_W)),
            _const_spec((1, ATTN_W)),
            _const_spec((1, ATTN_W)),
            _const_spec((2 * ATTN_W, ATTN_W)),
        ],
        out_specs=[tok(MIX_IN_W), tok(ATTN_W), tok(ATTN_W), tok(ATTN_W)],
        out_shape=[
            jax.ShapeDtypeStruct((t, MIX_IN_W), F32),
            jax.ShapeDtypeStruct((t, ATTN_W), BF16),
            jax.ShapeDtypeStruct((t, ATTN_W), BF16),
            jax.ShapeDtypeStruct((t, ATTN_W), BF16),
        ],
        compiler_params=pltpu.CompilerParams(
            dimension_semantics=("arbitrary",), vmem_limit_bytes=VMEM_LIMIT_BYTES),
        name="proj_in",
    )(x2d, g, w_in, q_gain, k_gain, ones_bd)


def _bias_table_body(rpb_ref, o_ref):
    g = pl.program_id(0)
    j = pl.program_id(1)
    qc = lax.broadcasted_iota(jnp.int32, (GRID_W, 2 * GRID_W), 0)
    lane = lax.broadcasted_iota(jnp.int32, (GRID_W, 2 * GRID_W), 1)
    kc = lane % GRID_W
    c0 = jnp.clip(qc - WIN_C // 2, 0, GRID_W - WIN_C)
    valid = (kc >= c0) & (kc < c0 + WIN_C)
    dc = kc - qc + (WIN_C - 1)
    second = lane >= GRID_W
    blocks = []
    for hh in range(HEAD_GROUP):
        head = g * HEAD_GROUP + hh
        t = jnp.zeros((GRID_W, 2 * GRID_W), F32)
        for c in range(2 * WIN_C - 1):
            val = jnp.where(second, rpb_ref[head, j + 1, c], rpb_ref[head, j, c])
            t = jnp.where(dc == c, val, t)
        blocks.append(jnp.where(valid, t, NEG))
    o_ref[0, 0] = jnp.concatenate(blocks, axis=0)


def _bias_table(rpb_l):
    return pl.pallas_call(
        _bias_table_body,
        grid=(N_GROUPS, N_DR_PAIRS),
        in_specs=[pl.BlockSpec(memory_space=pltpu.SMEM)],
        out_specs=pl.BlockSpec((1, 1, GROUP_W, 2 * GRID_W), lambda g, j: (g, j, 0, 0)),
        out_shape=jax.ShapeDtypeStruct((N_GROUPS, N_DR_PAIRS, GROUP_W, 2 * GRID_W), F32),
        compiler_params=pltpu.CompilerParams(dimension_semantics=("arbitrary", "arbitrary")),
        name="bias_table",
    )(rpb_l)


def _mix_body(x_ref, main_ref, left_ref, right_ref, q_ref, k_ref, v_ref, tab_ref,
              dw_ref, dwb_ref, lng_ref, lnb_ref, pw_ref, poolw_ref, pools_ref, wout_ref,
              o_ref, conv_ext, pool_ext, cat_ref):
    i = pl.program_id(1)
    n_blocks = pl.num_programs(1)
    seq = n_blocks * MIX_TOKENS

    def glu(blk):
        return blk[:, :CONV_W] * jax.nn.sigmoid(blk[:, CONV_W:2 * CONV_W])

    left = left_ref[0]
    right = right_ref[0]
    lmask = jnp.where(i > 0, 1.0, 0.0).astype(F32)
    rmask = jnp.where(i < n_blocks - 1, 1.0, 0.0).astype(F32)
    main = main_ref[0]
    conv_ext[0:HALO, :] = glu(left) * lmask
    conv_ext[HALO:HALO + MIX_TOKENS, :] = glu(main)
    conv_ext[HALO + MIX_TOKENS:, :] = glu(right) * rmask
    pool_ext[0:HALO, :] = left[:, 2 * CONV_W:] * lmask
    pool_ext[HALO:HALO + MIX_TOKENS, :] = main[:, 2 * CONV_W:]
    pool_ext[HALO + MIX_TOKENS:, :] = right[:, 2 * CONV_W:] * rmask

    lane128 = lax.broadcasted_iota(jnp.int32, (ROW_CHUNK, 128), 1)
    first_half = lane128 < POOL_GROUP
    dwb = dwb_ref[...]
    lng = lng_ref[...]
    lnb = lnb_ref[...]
    pools = pools_ref[...]
    pw = pw_ref[...]
    poolw = poolw_ref[...]

    for c in range(MIX_TOKENS // ROW_CHUNK):
        base = c * ROW_CHUNK
        acc = jnp.zeros((ROW_CHUNK, CONV_W), F32)
        for j in range(CONV_K):
            off = HALO + base + j - CONV_K // 2
            acc = acc + conv_ext[off:off + ROW_CHUNK, :] * dw_ref[j:j + 1, :]
        y = acc + dwb
        mu = jnp.mean(y, axis=-1, keepdims=True)
        yc = y - mu
        var = jnp.mean(yc * yc, axis=-1, keepdims=True)
        yn = yc * lax.rsqrt(var + EPS) * lng + lnb
        s = (yn * jax.nn.sigmoid(yn)).astype(BF16)
        c_out = jnp.dot(s, pw, preferred_element_type=F32)
        cat_ref[base:base + ROW_CHUNK, 0:CONV_W] = c_out.astype(BF16)

        def psum(lo, hi, col):
            tot = None
            for d in range(lo, hi):
                off = HALO + base + d
                piece = pool_ext[off:off + ROW_CHUNK, col * 128:(col + 1) * 128]
                tot = piece if tot is None else tot + piece
            return tot

        tpos = i * MIX_TOKENS + base + lax.broadcasted_iota(jnp.int32, (ROW_CHUNK, 128), 0)

        def count(w):
            lo = jnp.maximum(tpos - w // 2, 0)
            hi = jnp.minimum(tpos - w // 2 + w, seq)
            return (hi - lo).astype(F32)

        a2 = psum(-1, 1, 0)
        a4 = a2 + psum(-2, -1, 0) + psum(1, 2, 0)
        a8 = psum(-4, 4, 1)
        a16 = a8 + psum(-8, -4, 1) + psum(4, 8, 1)
        mean01 = jnp.where(first_half, a2 / count(2), a4 / count(4))
        mean23 = jnp.where(first_half, a8 / count(8), a16 / count(16))
        tok = pool_ext[HALO + base:HALO + base + ROW_CHUNK, :]
        mixed = (jnp.concatenate([mean01, mean23], axis=1) - tok).astype(BF16)
        p_out = jnp.dot(mixed, poolw, preferred_element_type=F32) * pools
        cat_ref[base:base + ROW_CHUNK, CONV_W:CONV_W + POOL_W] = p_out.astype(BF16)

    n_rows = n_blocks * MIX_ROWS
    lane_head = lax.broadcasted_iota(jnp.int32, (GRID_W, GROUP_W), 1) // HEAD_DIM

    def row_step(r, carry):
        row = i * MIX_ROWS + r
        r0 = jnp.clip(row - WIN_R // 2, 0, n_rows - WIN_R)
        j0 = r0 - row + (WIN_R - 1)
        kstart = pl.multiple_of(r0 * GRID_W, GRID_W)
        qstart = pl.multiple_of(r * GRID_W, GRID_W)
        for g in range(N_GROUPS):
            cols = slice(g * GROUP_W, (g + 1) * GROUP_W)
            q_r = q_ref[0, pl.ds(qstart, GRID_W), cols]
            kwin = k_ref[0, pl.ds(kstart, WIN_R * GRID_W), cols]
            vwin = v_ref[0, pl.ds(kstart, WIN_R * GRID_W), cols]
            zero = jnp.zeros_like(q_r)
            q_bd = jnp.concatenate(
                [jnp.where(lane_head == hh, q_r, zero) for hh in range(HEAD_GROUP)], axis=0)
            sc = lax.dot_general(q_bd, kwin, (((1,), (1,)), ((), ())),
                                 preferred_element_type=F32)
            bias = jnp.concatenate(
                [tab_ref[g, j0 + 2 * m] for m in range(WIN_R // 2)], axis=1)
            sc = sc + bias
            mx = jnp.max(sc, axis=-1, keepdims=True)
            p = jnp.exp(sc - mx)
            den = jnp.sum(p, axis=-1, keepdims=True)
            o_all = jnp.dot(p.astype(BF16), vwin, preferred_element_type=F32) / den
            out = jnp.zeros((GRID_W, GROUP_W), F32)
            for hh in range(HEAD_GROUP):
                out = jnp.where(lane_head == hh, o_all[hh * GRID_W:(hh + 1) * GRID_W, :], out)
            cat_ref[pl.ds(qstart, GRID_W),
                    CONV_W + POOL_W + g * GROUP_W:CONV_W + POOL_W + (g + 1) * GROUP_W] = (
                        out.astype(BF16))
        return carry

    lax.fori_loop(0, MIX_ROWS, row_step, 0, unroll=True)

    o_ref[0] = x_ref[0] + jnp.dot(cat_ref[...], wout_ref[...], preferred_element_type=F32)


def _mix(x3d, mix3d, q3d, k3d, v3d, table, dw, dwb, lng, lnb, pw, poolw, pools, wout):
    b, s, _ = x3d.shape
    n_blocks = s // MIX_TOKENS
    per_blk = MIX_TOKENS // HALO
    n_halo = s // HALO
    blk = lambda w: pl.BlockSpec((1, MIX_TOKENS, w), lambda bi, i: (bi, i, 0))
    full = lambda w: pl.BlockSpec((1, s, w), lambda bi, i: (bi, 0, 0))
    return pl.pallas_call(
        _mix_body,
        grid=(b, n_blocks),
        in_specs=[
            blk(D_MODEL),
            blk(MIX_IN_W),
            pl.BlockSpec((1, HALO, MIX_IN_W),
                         lambda bi, i: (bi, jnp.maximum(i * per_blk - 1, 0), 0)),
            pl.BlockSpec((1, HALO, MIX_IN_W),
                         lambda bi, i: (bi, jnp.minimum((i + 1) * per_blk, n_halo - 1), 0)),
            blk(ATTN_W),
            full(ATTN_W),
            full(ATTN_W),
            _const_spec(table.shape),
            _const_spec(dw.shape),
            _const_spec(dwb.shape),
            _const_spec(lng.shape),
            _const_spec(lnb.shape),
            _const_spec(pw.shape),
            _const_spec(poolw.shape),
            _const_spec(pools.shape),
            _const_spec(wout.shape),
        ],
        out_specs=blk(D_MODEL),
        out_shape=jax.ShapeDtypeStruct((b, s, D_MODEL), F32),
        scratch_shapes=[
            pltpu.VMEM((MIX_TOKENS + 2 * HALO, CONV_W), F32),
            pltpu.VMEM((MIX_TOKENS + 2 * HALO, POOL_W), F32),
            pltpu.VMEM((MIX_TOKENS, D_MODEL), BF16),
        ],
        compiler_params=pltpu.CompilerParams(
            dimension_semantics=("arbitrary", "arbitrary"), vmem_limit_bytes=VMEM_LIMIT_BYTES),
        name="mixer",
    )(x3d, mix3d, mix3d, mix3d, q3d, k3d, v3d, table, dw, dwb, lng, lnb, pw, poolw, pools, wout)


def _block_diag(blocks):
    n, r, c = blocks.shape
    eye = jnp.eye(n, dtype=blocks.dtype)
    return (eye[:, None, :, None] * blocks[:, :, None, :]).reshape(n * r, n * c)


def kernel(x, ffn1_norm, ffn1_gate, ffn1_up, ffn1_down, mix_norm, w_in, conv_dw, conv_dw_b,
           conv_ln_g, conv_ln_b, conv_pw, pool_w, pool_scale, q_norm, k_norm, rpb, w_out,
           ffn2_norm, ffn2_gate, ffn2_up, ffn2_down):
    b, s, d = x.shape
    assert d == D_MODEL and s % MIX_TOKENS == 0 and (b * s) % FFN_TOKENS == 0
    t = b * s
    row = lambda a: a.reshape(1, -1)
    head_ones = np.kron(np.eye(N_HEADS, dtype=np.float32),
                        np.ones((HEAD_DIM, HEAD_DIM), np.float32))
    ones_bd = jnp.asarray(np.concatenate([head_ones, head_ones], axis=0), dtype=BF16)

    x2d = x.reshape(t, d)
    for l in range(DEPTH):
        x2d = _ffn(x2d, row(ffn1_norm[l]), ffn1_gate[l].astype(BF16),
                   ffn1_up[l].astype(BF16), ffn1_down[l].astype(BF16))
        mix_in, q, k, v = _proj(
            x2d, row(mix_norm[l]), w_in[l].astype(BF16),
            row(jnp.tile(q_norm[l], N_HEADS)), row(jnp.tile(k_norm[l], N_HEADS)), ones_bd)
        table = _bias_table(rpb[l])
        x3d = _mix(
            x2d.reshape(b, s, d), mix_in.reshape(b, s, MIX_IN_W),
            q.reshape(b, s, ATTN_W), k.reshape(b, s, ATTN_W), v.reshape(b, s, ATTN_W),
            table, conv_dw[l], row(conv_dw_b[l]), row(conv_ln_g[l]), row(conv_ln_b[l]),
            conv_pw[l].astype(BF16), _block_diag(pool_w[l]).astype(BF16),
            row(pool_scale[l]), w_out[l].astype(BF16))
        x2d = _ffn(x3d.reshape(t, d), row(ffn2_norm[l]), ffn2_gate[l].astype(BF16),
                   ffn2_up[l].astype(BF16), ffn2_down[l].astype(BF16))
    return x2d.reshape(b, s, d)
```

```python
import functools

import numpy as np
import jax
import jax.numpy as jnp
from jax import lax
from jax.experimental import pallas as pl
from jax.experimental.pallas import tpu as pltpu

D_MODEL = 1024
DEPTH = 2
CONV_W = 256
POOL_W = 256
ATTN_W = 512
HEAD_DIM = 64
N_HEADS = 8
CONV_K = 31
POOL_WINDOWS = (2, 4, 8, 16)
POOL_GROUP = 64
GRID_W = 64
WIN_R = 8
WIN_C = 16
D_FF = 2816
IN_W = 2 * CONV_W + POOL_W + 3 * ATTN_W
MIX_IN_W = 2 * CONV_W + POOL_W
EPS = 1e-6
NEG = -1e30

F32 = jnp.float32
BF16 = jnp.bfloat16

VMEM_LIMIT_BYTES = 56 * 1024 * 1024

LANES = 128
SUBLANES = 8
FFN_TOKENS = 512
MXU_TILE = 256
FF_CHUNKS = (6 * MXU_TILE, 5 * MXU_TILE)
PROJ_TOKENS = 512
MIX_ROWS = 8
MIX_TOKENS = MIX_ROWS * GRID_W
HALO = 16
ROW_CHUNK = 64
HEAD_GROUP = 4
GROUP_W = HEAD_GROUP * HEAD_DIM
N_GROUPS = N_HEADS // HEAD_GROUP
N_DR = 2 * WIN_R - 1
N_DR_PAIRS = N_DR - 1


def _const_spec(shape):
    nd = len(shape)
    return pl.BlockSpec(shape, lambda *_: (0,) * nd, pipeline_mode=pl.Buffered(1))


def _rms(x, g):
    return x * lax.rsqrt(jnp.mean(x * x, axis=-1, keepdims=True) + EPS) * g


def _ffn_body(x_ref, g_ref, wg_ref, wu_ref, wd_ref, o_ref):
    x = x_ref[...]
    h = _rms(x, g_ref[...]).astype(BF16)
    y = None
    start = 0
    for width in FF_CHUNKS:
        sl = slice(start, start + width)
        start += width
        gate = jnp.dot(h, wg_ref[:, sl], preferred_element_type=F32)
        up = jnp.dot(h, wu_ref[:, sl], preferred_element_type=F32)
        act = (gate * jax.nn.sigmoid(gate) * up).astype(BF16)
        part = jnp.dot(act, wd_ref[sl, :], preferred_element_type=F32)
        y = part if y is None else y + part
    o_ref[...] = x + 0.5 * y


def _ffn(x2d, g, wg, wu, wd):
    t = x2d.shape[0]
    return pl.pallas_call(
        _ffn_body,
        grid=(t // FFN_TOKENS,),
        in_specs=[
            pl.BlockSpec((FFN_TOKENS, D_MODEL), lambda i: (i, 0)),
            _const_spec((1, D_MODEL)),
            _const_spec((D_MODEL, D_FF)),
            _const_spec((D_MODEL, D_FF)),
            _const_spec((D_FF, D_MODEL)),
        ],
        out_specs=pl.BlockSpec((FFN_TOKENS, D_MODEL), lambda i: (i, 0)),
        out_shape=jax.ShapeDtypeStruct((t, D_MODEL), F32),
        compiler_params=pltpu.CompilerParams(
            dimension_semantics=("arbitrary",), vmem_limit_bytes=VMEM_LIMIT_BYTES),
        name="ffn",
    )(x2d, g, wg, wu, wd)


def _head_norm(x, gain):
    first = lax.broadcasted_iota(jnp.int32, (1, LANES), 1) < HEAD_DIM
    tiles = []
    for v in range(x.shape[1] // LANES):
        xt = x[:, v * LANES:(v + 1) * LANES]
        sq = xt * xt
        ss0 = jnp.sum(jnp.where(first, sq, 0.0), axis=-1, keepdims=True)
        ss1 = jnp.sum(jnp.where(first, 0.0, sq), axis=-1, keepdims=True)
        inv = jnp.where(first, lax.rsqrt(ss0 * (1.0 / HEAD_DIM) + EPS),
                        lax.rsqrt(ss1 * (1.0 / HEAD_DIM) + EPS))
        tiles.append(xt * inv)
    return jnp.concatenate(tiles, axis=1) * gain


def _proj_body(x_ref, g_ref, w_ref, qg_ref, kg_ref, mix_ref, q_ref, k_ref, v_ref):
    h = _rms(x_ref[...], g_ref[...]).astype(BF16)
    u = jnp.dot(h, w_ref[...], preferred_element_type=F32)
    mix_ref[...] = u[:, :MIX_IN_W]
    q = u[:, MIX_IN_W:MIX_IN_W + ATTN_W]
    k = u[:, MIX_IN_W + ATTN_W:MIX_IN_W + 2 * ATTN_W]
    v = u[:, MIX_IN_W + 2 * ATTN_W:]
    q_ref[...] = (_head_norm(q, qg_ref[...]) * (HEAD_DIM ** -0.5)).astype(BF16)
    k_ref[...] = _head_norm(k, kg_ref[...]).astype(BF16)
    v_ref[...] = v.astype(BF16)


def _proj(x2d, g, w_in, q_gain, k_gain):
    t = x2d.shape[0]
    tok = lambda w: pl.BlockSpec((PROJ_TOKENS, w), lambda i: (i, 0))
    return pl.pallas_call(
        _proj_body,
        grid=(t // PROJ_TOKENS,),
        in_specs=[
            tok(D_MODEL),
            _const_spec((1, D_MODEL)),
            _const_spec((D_MODEL, IN_W)),
            _const_spec((1, ATTN_W)),
            _const_spec((1, ATTN_W)),
        ],
        out_specs=[tok(MIX_IN_W), tok(ATTN_W), tok(ATTN_W), tok(ATTN_W)],
        out_shape=[
            jax.ShapeDtypeStruct((t, MIX_IN_W), F32),
            jax.ShapeDtypeStruct((t, ATTN_W), BF16),
            jax.ShapeDtypeStruct((t, ATTN_W), BF16),
            jax.ShapeDtypeStruct((t, ATTN_W), BF16),
        ],
        compiler_params=pltpu.CompilerParams(
            dimension_semantics=("arbitrary",), vmem_limit_bytes=VMEM_LIMIT_BYTES),
        name="proj_in",
    )(x2d, g, w_in, q_gain, k_gain)


def _bias_table_body(rpb_ref, o_ref):
    g = pl.program_id(0)
    j = pl.program_id(1)
    qc = lax.broadcasted_iota(jnp.int32, (GRID_W, 2 * GRID_W), 0)
    lane = lax.broadcasted_iota(jnp.int32, (GRID_W, 2 * GRID_W), 1)
    kc = lane % GRID_W
    c0 = jnp.clip(qc - WIN_C // 2, 0, GRID_W - WIN_C)
    valid = (kc >= c0) & (kc < c0 + WIN_C)
    dc = kc - qc + (WIN_C - 1)
    second = lane >= GRID_W
    blocks = []
    for hh in range(HEAD_GROUP):
        head = g * HEAD_GROUP + hh
        t = jnp.zeros((GRID_W, 2 * GRID_W), F32)
        for c in range(2 * WIN_C - 1):
            val = jnp.where(second, rpb_ref[head, j + 1, c], rpb_ref[head, j, c])
            t = jnp.where(dc == c, val, t)
        blocks.append(jnp.where(valid, t, NEG))
    o_ref[0, 0] = jnp.concatenate(blocks, axis=0)


def _bias_table(rpb_l):
    return pl.pallas_call(
        _bias_table_body,
        grid=(N_GROUPS, N_DR_PAIRS),
        in_specs=[pl.BlockSpec(memory_space=pltpu.SMEM)],
        out_specs=pl.BlockSpec((1, 1, GROUP_W, 2 * GRID_W), lambda g, j: (g, j, 0, 0)),
        out_shape=jax.ShapeDtypeStruct((N_GROUPS, N_DR_PAIRS, GROUP_W, 2 * GRID_W), F32),
        compiler_params=pltpu.CompilerParams(dimension_semantics=("arbitrary", "arbitrary")),
        name="bias_table",
    )(rpb_l)


def _mix_body(x_ref, main_ref, left_ref, right_ref, q_ref, k_ref, v_ref, tab_ref,
              dw_ref, dwb_ref, lng_ref, lnb_ref, pw_ref, poolw_ref, pools_ref, wout_ref,
              o_ref, conv_ext, pool_ext, cat_ref):
    i = pl.program_id(1)
    n_blocks = pl.num_programs(1)
    seq = n_blocks * MIX_TOKENS

    def glu(blk):
        return blk[:, :CONV_W] * jax.nn.sigmoid(blk[:, CONV_W:2 * CONV_W])

    left = left_ref[0]
    right = right_ref[0]
    lmask = jnp.where(i > 0, 1.0, 0.0).astype(F32)
    rmask = jnp.where(i < n_blocks - 1, 1.0, 0.0).astype(F32)
    main = main_ref[0]
    conv_ext[0:HALO, :] = glu(left) * lmask
    conv_ext[HALO:HALO + MIX_TOKENS, :] = glu(main)
    conv_ext[HALO + MIX_TOKENS:, :] = glu(right) * rmask
    pool_ext[0:HALO, :] = left[:, 2 * CONV_W:] * lmask
    pool_ext[HALO:HALO + MIX_TOKENS, :] = main[:, 2 * CONV_W:]
    pool_ext[HALO + MIX_TOKENS:, :] = right[:, 2 * CONV_W:] * rmask

    lane128 = lax.broadcasted_iota(jnp.int32, (ROW_CHUNK, 128), 1)
    first_half = lane128 < POOL_GROUP
    dwb = dwb_ref[...]
    lng = lng_ref[...]
    lnb = lnb_ref[...]
    pools = pools_ref[...]
    pw = pw_ref[...]
    poolw = poolw_ref[...]

    for c in range(MIX_TOKENS // ROW_CHUNK):
        base = c * ROW_CHUNK
        y = dwb
        for b in range(SUBLANES):
            part = None
            for a in range(-(HALO // SUBLANES), HALO // SUBLANES):
                j = SUBLANES * a + b + CONV_K // 2
                if 0 <= j < CONV_K:
                    off = HALO + base + SUBLANES * a
                    term = conv_ext[off:off + ROW_CHUNK + SUBLANES, :] * dw_ref[j:j + 1, :]
                    part = term if part is None else part + term
            y = y + part[b:b + ROW_CHUNK, :]
        mu = jnp.mean(y, axis=-1, keepdims=True)
        yc = y - mu
        var = jnp.mean(yc * yc, axis=-1, keepdims=True)
        yn = yc * lax.rsqrt(var + EPS) * lng + lnb
        s = (yn * jax.nn.sigmoid(yn)).astype(BF16)
        c_out = jnp.dot(s, pw, preferred_element_type=F32)
        cat_ref[base:base + ROW_CHUNK, 0:CONV_W] = c_out.astype(BF16)

        def psum(lo, hi, col):
            tot = None
            for d in range(lo, hi):
                off = HALO + base + d
                piece = pool_ext[off:off + ROW_CHUNK, col * 128:(col + 1) * 128]
                tot = piece if tot is None else tot + piece
            return tot

        clipped = c == 0 or c == MIX_TOKENS // ROW_CHUNK - 1
        tpos = i * MIX_TOKENS + base + lax.broadcasted_iota(jnp.int32, (ROW_CHUNK, 128), 0)

        def mean(total, w):
            if not clipped:
                return total * (1.0 / w)
            lo = jnp.maximum(tpos - w // 2, 0)
            hi = jnp.minimum(tpos - w // 2 + w, seq)
            return total / (hi - lo).astype(F32)

        a2 = psum(-1, 1, 0)
        a4 = a2 + psum(-2, -1, 0) + psum(1, 2, 0)
        a8 = psum(-4, 4, 1)
        a16 = a8 + psum(-8, -4, 1) + psum(4, 8, 1)
        mean01 = jnp.where(first_half, mean(a2, 2), mean(a4, 4))
        mean23 = jnp.where(first_half, mean(a8, 8), mean(a16, 16))
        tok = pool_ext[HALO + base:HALO + base + ROW_CHUNK, :]
        mixed = (jnp.concatenate([mean01, mean23], axis=1) - tok).astype(BF16)
        p_out = jnp.dot(mixed, poolw, preferred_element_type=F32) * pools
        cat_ref[base:base + ROW_CHUNK, CONV_W:CONV_W + POOL_W] = p_out.astype(BF16)

    n_rows = n_blocks * MIX_ROWS
    lane_head = lax.broadcasted_iota(jnp.int32, (GRID_W, GROUP_W), 1) // HEAD_DIM

    def row_step(r, carry):
        row = i * MIX_ROWS + r
        r0 = jnp.clip(row - WIN_R // 2, 0, n_rows - WIN_R)
        j0 = r0 - row + (WIN_R - 1)
        kstart = pl.multiple_of(r0 * GRID_W, GRID_W)
        qstart = pl.multiple_of(r * GRID_W, GRID_W)
        for g in range(N_GROUPS):
            cols = slice(g * GROUP_W, (g + 1) * GROUP_W)
            q_r = q_ref[0, pl.ds(qstart, GRID_W), cols]
            kwin = k_ref[0, pl.ds(kstart, WIN_R * GRID_W), cols]
            vwin = v_ref[0, pl.ds(kstart, WIN_R * GRID_W), cols]
            zero = jnp.zeros_like(q_r)
            q_bd = jnp.concatenate(
                [jnp.where(lane_head == hh, q_r, zero) for hh in range(HEAD_GROUP)], axis=0)
            sc = lax.dot_general(q_bd, kwin, (((1,), (1,)), ((), ())),
                                 preferred_element_type=F32)
            bias = jnp.concatenate(
                [tab_ref[g, j0 + 2 * m] for m in range(WIN_R // 2)], axis=1)
            sc = sc + bias
            mx = jnp.max(sc, axis=-1, keepdims=True)
            p = jnp.exp(sc - mx)
            den = jnp.sum(p, axis=-1, keepdims=True)
            o_all = jnp.dot(p.astype(BF16), vwin, preferred_element_type=F32) / den
            out = jnp.zeros((GRID_W, GROUP_W), F32)
            for hh in range(HEAD_GROUP):
                out = jnp.where(lane_head == hh, o_all[hh * GRID_W:(hh + 1) * GRID_W, :], out)
            cat_ref[pl.ds(qstart, GRID_W),
                    CONV_W + POOL_W + g * GROUP_W:CONV_W + POOL_W + (g + 1) * GROUP_W] = (
                        out.astype(BF16))
        return carry

    lax.fori_loop(0, MIX_ROWS, row_step, 0, unroll=True)

    o_ref[0] = x_ref[0] + jnp.dot(cat_ref[...], wout_ref[...], preferred_element_type=F32)


def _mix(x3d, mix3d, q3d, k3d, v3d, table, dw, dwb, lng, lnb, pw, poolw, pools, wout):
    b, s, _ = x3d.shape
    n_blocks = s // MIX_TOKENS
    per_blk = MIX_TOKENS // HALO
    n_halo = s // HALO
    blk = lambda w: pl.BlockSpec((1, MIX_TOKENS, w), lambda bi, i: (bi, i, 0))
    full = lambda w: pl.BlockSpec((1, s, w), lambda bi, i: (bi, 0, 0))
    return pl.pallas_call(
        _mix_body,
        grid=(b, n_blocks),
        in_specs=[
            blk(D_MODEL),
            blk(MIX_IN_W),
            pl.BlockSpec((1, HALO, MIX_IN_W),
                         lambda bi, i: (bi, jnp.maximum(i * per_blk - 1, 0), 0)),
            pl.BlockSpec((1, HALO, MIX_IN_W),
                         lambda bi, i: (bi, jnp.minimum((i + 1) * per_blk, n_halo - 1), 0)),
            blk(ATTN_W),
            full(ATTN_W),
            full(ATTN_W),
            _const_spec(table.shape),
            _const_spec(dw.shape),
            _const_spec(dwb.shape),
            _const_spec(lng.shape),
            _const_spec(lnb.shape),
            _const_spec(pw.shape),
            _const_spec(poolw.shape),
            _const_spec(pools.shape),
            _const_spec(wout.shape),
        ],
        out_specs=blk(D_MODEL),
        out_shape=jax.ShapeDtypeStruct((b, s, D_MODEL), F32),
        scratch_shapes=[
            pltpu.VMEM((MIX_TOKENS + 2 * HALO, CONV_W), F32),
            pltpu.VMEM((MIX_TOKENS + 2 * HALO, POOL_W), F32),
            pltpu.VMEM((MIX_TOKENS, D_MODEL), BF16),
        ],
        compiler_params=pltpu.CompilerParams(
            dimension_semantics=("arbitrary", "arbitrary"), vmem_limit_bytes=VMEM_LIMIT_BYTES),
        name="mixer",
    )(x3d, mix3d, mix3d, mix3d, q3d, k3d, v3d, table, dw, dwb, lng, lnb, pw, poolw, pools, wout)


def _block_diag(blocks):
    n, r, c = blocks.shape
    eye = jnp.eye(n, dtype=blocks.dtype)
    return (eye[:, None, :, None] * blocks[:, :, None, :]).reshape(n * r, n * c)


def kernel(x, ffn1_norm, ffn1_gate, ffn1_up, ffn1_down, mix_norm, w_in, conv_dw, conv_dw_b,
           conv_ln_g, conv_ln_b, conv_pw, pool_w, pool_scale, q_norm, k_norm, rpb, w_out,
           ffn2_norm, ffn2_gate, ffn2_up, ffn2_down):
    b, s, d = x.shape
    assert d == D_MODEL and s % MIX_TOKENS == 0 and (b * s) % FFN_TOKENS == 0
    t = b * s
    row = lambda a: a.reshape(1, -1)

    x2d = x.reshape(t, d)
    for l in range(DEPTH):
        x2d = _ffn(x2d, row(ffn1_norm[l]), ffn1_gate[l].astype(BF16),
                   ffn1_up[l].astype(BF16), ffn1_down[l].astype(BF16))
        mix_in, q, k, v = _proj(
            x2d, row(mix_norm[l]), w_in[l].astype(BF16),
            row(jnp.tile(q_norm[l], N_HEADS)), row(jnp.tile(k_norm[l], N_HEADS)))
        table = _bias_table(rpb[l])
        x3d = _mix(
            x2d.reshape(b, s, d), mix_in.reshape(b, s, MIX_IN_W),
            q.reshape(b, s, ATTN_W), k.reshape(b, s, ATTN_W), v.reshape(b, s, ATTN_W),
            table, conv_dw[l], row(conv_dw_b[l]), row(conv_ln_g[l]), row(conv_ln_b[l]),
            conv_pw[l].astype(BF16), _block_diag(pool_w[l]).astype(BF16),
            row(pool_scale[l]), w_out[l].astype(BF16))
        x2d = _ffn(x3d.reshape(t, d), row(ffn2_norm[l]), ffn2_gate[l].astype(BF16),
                   ffn2_up[l].astype(BF16), ffn2_down[l].astype(BF16))
    return x2d.reshape(b, s, d)
```

```python
import functools

import numpy as np
import jax
import jax.numpy as jnp
from jax import lax
from jax.experimental import pallas as pl
from jax.experimental.pallas import tpu as pltpu

D_MODEL = 1024
DEPTH = 2
CONV_W = 256
POOL_W = 256
ATTN_W = 512
HEAD_DIM = 64
N_HEADS = 8
CONV_K = 31
POOL_WINDOWS = (2, 4, 8, 16)
POOL_GROUP = 64
GRID_W = 64
WIN_R = 8
WIN_C = 16
D_FF = 2816
IN_W = 2 * CONV_W + POOL_W + 3 * ATTN_W
MIX_IN_W = 2 * CONV_W + POOL_W
EPS = 1e-6
NEG = -1e30

F32 = jnp.float32
BF16 = jnp.bfloat16

VMEM_LIMIT_BYTES = 56 * 1024 * 1024

LANES = 128
SUBLANES = 8
FFN_TOKENS = 1024
MXU_TILE = 256
FF_CHUNKS = (6 * MXU_TILE, 5 * MXU_TILE)
PROJ_TOKENS = 512
MIX_ROWS = 8
MIX_TOKENS = MIX_ROWS * GRID_W
HALO = 16
ROW_CHUNK = GRID_W
assert sum(FF_CHUNKS) == D_FF and ROW_CHUNK >= max(POOL_WINDOWS) // 2
HEAD_GROUP = 4
GROUP_W = HEAD_GROUP * HEAD_DIM
N_GROUPS = N_HEADS // HEAD_GROUP
N_DR = 2 * WIN_R - 1
N_DR_PAIRS = N_DR - 1


def _const_spec(shape):
    nd = len(shape)
    return pl.BlockSpec(shape, lambda *_: (0,) * nd, pipeline_mode=pl.Buffered(1))


def _rms(x, g):
    return x * lax.rsqrt(jnp.mean(x * x, axis=-1, keepdims=True) + EPS) * g


def _ffn_compute(x, g_ref, wg_ref, wu_ref, wd_ref):
    h = _rms(x, g_ref[...]).astype(BF16)
    y = None
    start = 0
    for width in FF_CHUNKS:
        sl = slice(start, start + width)
        start += width
        gate = jnp.dot(h, wg_ref[:, sl], preferred_element_type=F32)
        up = jnp.dot(h, wu_ref[:, sl], preferred_element_type=F32)
        act = (gate * jax.nn.sigmoid(gate) * up).astype(BF16)
        part = jnp.dot(act, wd_ref[sl, :], preferred_element_type=F32)
        y = part if y is None else y + part
    return x + 0.5 * y


def _ffn_body(x_ref, g_ref, wg_ref, wu_ref, wd_ref, o_ref):
    o_ref[...] = _ffn_compute(x_ref[...], g_ref, wg_ref, wu_ref, wd_ref)


def _ffn(x2d, g, wg, wu, wd):
    t = x2d.shape[0]
    return pl.pallas_call(
        _ffn_body,
        grid=(t // FFN_TOKENS,),
        in_specs=[
            pl.BlockSpec((FFN_TOKENS, D_MODEL), lambda i: (i, 0)),
            _const_spec((1, D_MODEL)),
            _const_spec((D_MODEL, D_FF)),
            _const_spec((D_MODEL, D_FF)),
            _const_spec((D_FF, D_MODEL)),
        ],
        out_specs=pl.BlockSpec((FFN_TOKENS, D_MODEL), lambda i: (i, 0)),
        out_shape=jax.ShapeDtypeStruct((t, D_MODEL), F32),
        compiler_params=pltpu.CompilerParams(
            dimension_semantics=("arbitrary",), vmem_limit_bytes=VMEM_LIMIT_BYTES),
        name="ffn",
    )(x2d, g, wg, wu, wd)


def _head_norm(x, gain):
    first = lax.broadcasted_iota(jnp.int32, (1, LANES), 1) < HEAD_DIM
    tiles = []
    for v in range(x.shape[1] // LANES):
        xt = x[:, v * LANES:(v + 1) * LANES]
        sq = xt * xt
        ss0 = jnp.sum(jnp.where(first, sq, 0.0), axis=-1, keepdims=True)
        ss1 = jnp.sum(jnp.where(first, 0.0, sq), axis=-1, keepdims=True)
        inv = jnp.where(first, lax.rsqrt(ss0 * (1.0 / HEAD_DIM) + EPS),
                        lax.rsqrt(ss1 * (1.0 / HEAD_DIM) + EPS))
        tiles.append(xt * inv)
    return jnp.concatenate(tiles, axis=1) * gain


def _proj_body(x_ref, g_ref, w_ref, qg_ref, kg_ref, mix_ref, q_ref, k_ref, v_ref):
    h = _rms(x_ref[...], g_ref[...]).astype(BF16)
    u = jnp.dot(h, w_ref[...], preferred_element_type=F32)
    mix_ref[...] = u[:, :MIX_IN_W]
    q = u[:, MIX_IN_W:MIX_IN_W + ATTN_W]
    k = u[:, MIX_IN_W + ATTN_W:MIX_IN_W + 2 * ATTN_W]
    v = u[:, MIX_IN_W + 2 * ATTN_W:]
    q_ref[...] = (_head_norm(q, qg_ref[...]) * (HEAD_DIM ** -0.5)).astype(BF16)
    k_ref[...] = _head_norm(k, kg_ref[...]).astype(BF16)
    v_ref[...] = v.astype(BF16)


def _proj(x2d, g, w_in, q_gain, k_gain):
    t = x2d.shape[0]
    tok = lambda w: pl.BlockSpec((PROJ_TOKENS, w), lambda i: (i, 0))
    return pl.pallas_call(
        _proj_body,
        grid=(t // PROJ_TOKENS,),
        in_specs=[
            tok(D_MODEL),
            _const_spec((1, D_MODEL)),
            _const_spec((D_MODEL, IN_W)),
            _const_spec((1, ATTN_W)),
            _const_spec((1, ATTN_W)),
        ],
        out_specs=[tok(MIX_IN_W), tok(ATTN_W), tok(ATTN_W), tok(ATTN_W)],
        out_shape=[
            jax.ShapeDtypeStruct((t, MIX_IN_W), F32),
            jax.ShapeDtypeStruct((t, ATTN_W), BF16),
            jax.ShapeDtypeStruct((t, ATTN_W), BF16),
            jax.ShapeDtypeStruct((t, ATTN_W), BF16),
        ],
        compiler_params=pltpu.CompilerParams(
            dimension_semantics=("arbitrary",), vmem_limit_bytes=VMEM_LIMIT_BYTES),
        name="proj_in",
    )(x2d, g, w_in, q_gain, k_gain)


def _bias_table_body(rpb_ref, o_ref):
    g = pl.program_id(0)
    j = pl.program_id(1)
    qc = lax.broadcasted_iota(jnp.int32, (GRID_W, 2 * GRID_W), 0)
    lane = lax.broadcasted_iota(jnp.int32, (GRID_W, 2 * GRID_W), 1)
    kc = lane % GRID_W
    c0 = jnp.clip(qc - WIN_C // 2, 0, GRID_W - WIN_C)
    valid = (kc >= c0) & (kc < c0 + WIN_C)
    dc = kc - qc + (WIN_C - 1)
    second = lane >= GRID_W
    blocks = []
    for hh in range(HEAD_GROUP):
        head = g * HEAD_GROUP + hh
        t = jnp.zeros((GRID_W, 2 * GRID_W), F32)
        for c in range(2 * WIN_C - 1):
            val = jnp.where(second, rpb_ref[head, j + 1, c], rpb_ref[head, j, c])
            t = jnp.where(dc == c, val, t)
        blocks.append(jnp.where(valid, t, NEG))
    o_ref[0, 0] = jnp.concatenate(blocks, axis=0)


def _bias_table(rpb_l):
    return pl.pallas_call(
        _bias_table_body,
        grid=(N_GROUPS, N_DR_PAIRS),
        in_specs=[pl.BlockSpec(memory_space=pltpu.SMEM)],
        out_specs=pl.BlockSpec((1, 1, GROUP_W, 2 * GRID_W), lambda g, j: (g, j, 0, 0)),
        out_shape=jax.ShapeDtypeStruct((N_GROUPS, N_DR_PAIRS, GROUP_W, 2 * GRID_W), F32),
        compiler_params=pltpu.CompilerParams(dimension_semantics=("arbitrary", "arbitrary")),
        name="bias_table",
    )(rpb_l)


def _mix_body(x_ref, main_ref, left_ref, right_ref, q_ref, k_ref, v_ref, tab_ref,
              dw_ref, dwb_ref, lng_ref, lnb_ref, pw_ref, poolw_ref, pools_ref, wout_ref,
              o_ref, conv_ext, pool_ext, cat_ref):
    i = pl.program_id(1)
    n_blocks = pl.num_programs(1)
    seq = n_blocks * MIX_TOKENS

    def glu(blk):
        return blk[:, :CONV_W] * jax.nn.sigmoid(blk[:, CONV_W:2 * CONV_W])

    left = left_ref[0]
    right = right_ref[0]
    lmask = jnp.where(i > 0, 1.0, 0.0).astype(F32)
    rmask = jnp.where(i < n_blocks - 1, 1.0, 0.0).astype(F32)
    main = main_ref[0]
    conv_ext[0:HALO, :] = glu(left) * lmask
    conv_ext[HALO:HALO + MIX_TOKENS, :] = glu(main)
    conv_ext[HALO + MIX_TOKENS:, :] = glu(right) * rmask
    pool_ext[0:HALO, :] = left[:, 2 * CONV_W:] * lmask
    pool_ext[HALO:HALO + MIX_TOKENS, :] = main[:, 2 * CONV_W:]
    pool_ext[HALO + MIX_TOKENS:, :] = right[:, 2 * CONV_W:] * rmask

    lane128 = lax.broadcasted_iota(jnp.int32, (ROW_CHUNK, 128), 1)
    first_half = lane128 < POOL_GROUP
    dwb = dwb_ref[...]
    lng = lng_ref[...]
    lnb = lnb_ref[...]
    pools = pools_ref[...]
    pw = pw_ref[...]
    poolw = poolw_ref[...]

    def conv_taps(c, residues, y):
        base = c * ROW_CHUNK
        for b in residues:
            part = None
            for a in range(-(HALO // SUBLANES), HALO // SUBLANES):
                j = SUBLANES * a + b + CONV_K // 2
                if 0 <= j < CONV_K:
                    off = HALO + base + SUBLANES * a
                    tap = jnp.concatenate(
                        [dw_ref[v, pl.ds(j, ROW_CHUNK + SUBLANES, stride=0), :]
                         for v in range(CONV_W // LANES)], axis=1)
                    term = conv_ext[off:off + ROW_CHUNK + SUBLANES, :] * tap
                    part = term if part is None else part + term
            y = y + part[b:b + ROW_CHUNK, :]
        return y

    def conv_finish(c, y):
        base = c * ROW_CHUNK
        mu = jnp.mean(y, axis=-1, keepdims=True)
        yc = y - mu
        var = jnp.mean(yc * yc, axis=-1, keepdims=True)
        yn = yc * lax.rsqrt(var + EPS) * lng + lnb
        swish = (yn * jax.nn.sigmoid(yn)).astype(BF16)
        c_out = jnp.dot(swish, pw, preferred_element_type=F32)
        cat_ref[base:base + ROW_CHUNK, 0:CONV_W] = c_out.astype(BF16)

    def pool_chunk(c):
        base = c * ROW_CHUNK

        margin = max(POOL_WINDOWS) // 2
        rows = ROW_CHUNK + 2 * margin
        lo_row = HALO + base - margin

        def after(x, d):
            return pltpu.roll(x, (-d) % rows, 0)

        def core(x):
            return x[margin:margin + ROW_CHUNK, :]

        e01 = pool_ext[lo_row:lo_row + rows, 0:LANES]
        e23 = pool_ext[lo_row:lo_row + rows, LANES:2 * LANES]
        pair01 = e01 + after(e01, 1)
        a2 = core(e01 + after(e01, -1))
        a4 = core(pair01 + after(pair01, -2))
        pair23 = e23 + after(e23, 1)
        quad23 = pair23 + after(pair23, 2)
        a8 = core(quad23 + after(quad23, -4))
        oct23 = quad23 + after(quad23, 4)
        a16 = oct23[0:ROW_CHUNK, :] + core(oct23)

        clipped = c == 0 or c == MIX_TOKENS // ROW_CHUNK - 1
        tpos = i * MIX_TOKENS + base + lax.broadcasted_iota(jnp.int32, (ROW_CHUNK, 128), 0)

        def mean(total, w):
            if not clipped:
                return total * (1.0 / w)
            lo = jnp.maximum(tpos - w // 2, 0)
            hi = jnp.minimum(tpos - w // 2 + w, seq)
            return total / (hi - lo).astype(F32)

        mean01 =jnp.where(first_half, mean(a2, 2), mean(a4, 4))
        mean23 = jnp.where(first_half, mean(a8, 8), mean(a16, 16))
        tok = pool_ext[HALO + base:HALO + base + ROW_CHUNK, :]
        mixed = (jnp.concatenate([mean01, mean23], axis=1) - tok).astype(BF16)
        p_out = jnp.dot(mixed, poolw, preferred_element_type=F32) * pools
        cat_ref[base:base + ROW_CHUNK, CONV_W:CONV_W + POOL_W] = p_out.astype(BF16)

    n_rows = n_blocks * MIX_ROWS
    lane_head = lax.broadcasted_iota(jnp.int32, (GRID_W, GROUP_W), 1) // HEAD_DIM

    def window(r):
        row = i * MIX_ROWS + r
        r0 = jnp.clip(row - WIN_R // 2, 0, n_rows - WIN_R)
        return pl.multiple_of(r0 * GRID_W, GRID_W), r0 - row + (WIN_R - 1)

    def att_scores(r, g):
        kstart, j0 = window(r)
        cols = slice(g * GROUP_W, (g + 1) * GROUP_W)
        q_r = q_ref[0, r * GRID_W:(r + 1) * GRID_W, cols]
        kwin = k_ref[0, pl.ds(kstart, WIN_R * GRID_W), cols]
        zero = jnp.zeros_like(q_r)
        q_bd = jnp.concatenate(
            [jnp.where(lane_head == hh, q_r, zero) for hh in range(HEAD_GROUP)], axis=0)
        return lax.dot_general(q_bd, kwin, (((1,), (1,)), ((), ())),
                               preferred_element_type=F32)

    def att_softmax(r, g, sc):
        _, j0 = window(r)
        bias = jnp.concatenate(
            [tab_ref[g, j0 + 2 * m] for m in range(WIN_R // 2)], axis=1)
        sc = sc + bias
        mx = jnp.max(sc, axis=-1, keepdims=True)
        p = jnp.exp(sc - mx)
        return p.astype(BF16), jnp.sum(p, axis=-1, keepdims=True)

    def att_values(r, g, p, den):
        kstart, _ = window(r)
        cols = slice(g * GROUP_W, (g + 1) * GROUP_W)
        vwin = v_ref[0, pl.ds(kstart, WIN_R * GRID_W), cols]
        return jnp.dot(p, vwin, preferred_element_type=F32) / den

    def att_finish(r, g, o_all):
        out = jnp.zeros((GRID_W, GROUP_W), F32)
        for hh in range(HEAD_GROUP):
            out = jnp.where(lane_head == hh, o_all[hh * GRID_W:(hh + 1) * GRID_W, :], out)
        cat_ref[r * GRID_W:(r + 1) * GRID_W,
                CONV_W + POOL_W + g * GROUP_W:CONV_W + POOL_W + (g + 1) * GROUP_W] = (
                    out.astype(BF16))

    for c in range(MIX_TOKENS // ROW_CHUNK):
        conv_finish(c, conv_taps(c, range(SUBLANES), dwb))
        pool_chunk(c)
    units = [(r, g) for r in range(MIX_ROWS) for g in range(N_GROUPS)]
    scores = att_scores(*units[0])
    for k, unit in enumerate(units):
        upcoming = att_scores(*units[k + 1]) if k + 1 < len(units) else None
        p, den = att_softmax(*unit, scores)
        att_finish(*unit, att_values(*unit, p, den))
        scores = upcoming

    o_ref[0] = x_ref[0] + jnp.dot(cat_ref[...], wout_ref[...], preferred_element_type=F32)


def _mix(x3d, mix3d, q3d, k3d, v3d, table, dw, dwb, lng, lnb, pw, poolw, pools, wout):
    b, s, _ = x3d.shape
    n_blocks = s // MIX_TOKENS
    per_blk = MIX_TOKENS // HALO
    n_halo = s // HALO
    blk = lambda w: pl.BlockSpec((1, MIX_TOKENS, w), lambda bi, i: (bi, i, 0))
    full = lambda w: pl.BlockSpec((1, s, w), lambda bi, i: (bi, 0, 0))
    return pl.pallas_call(
        _mix_body,
        grid=(b, n_blocks),
        in_specs=[
            blk(D_MODEL),
            blk(MIX_IN_W),
            pl.BlockSpec((1, HALO, MIX_IN_W),
                         lambda bi, i: (bi, jnp.maximum(i * per_blk - 1, 0), 0)),
            pl.BlockSpec((1, HALO, MIX_IN_W),
                         lambda bi, i: (bi, jnp.minimum((i + 1) * per_blk, n_halo - 1), 0)),
            blk(ATTN_W),
            full(ATTN_W),
            full(ATTN_W),
            _const_spec(table.shape),
            _const_spec(dw.shape),
            _const_spec(dwb.shape),
            _const_spec(lng.shape),
            _const_spec(lnb.shape),
            _const_spec(pw.shape),
            _const_spec(poolw.shape),
            _const_spec(pools.shape),
            _const_spec(wout.shape),
        ],
        out_specs=blk(D_MODEL),
        out_shape=jax.ShapeDtypeStruct((b, s, D_MODEL), F32),
        scratch_shapes=[
            pltpu.VMEM((MIX_TOKENS + 2 * HALO, CONV_W), F32),
            pltpu.VMEM((MIX_TOKENS + 2 * HALO, POOL_W), F32),
            pltpu.VMEM((MIX_TOKENS, D_MODEL), BF16),
        ],
        compiler_params=pltpu.CompilerParams(
            dimension_semantics=("arbitrary", "arbitrary"), vmem_limit_bytes=VMEM_LIMIT_BYTES),
        name="mixer",
    )(x3d, mix3d, mix3d, mix3d, q3d, k3d, v3d, table, dw, dwb, lng, lnb, pw, poolw, pools, wout)


def _block_diag(blocks):
    n, r, c = blocks.shape
    eye = jnp.eye(n, dtype=blocks.dtype)
    return (eye[:, None, :, None] * blocks[:, :, None, :]).reshape(n * r, n * c)


def kernel(x, ffn1_norm, ffn1_gate, ffn1_up, ffn1_down, mix_norm, w_in, conv_dw, conv_dw_b,
           conv_ln_g, conv_ln_b, conv_pw, pool_w, pool_scale, q_norm, k_norm, rpb, w_out,
           ffn2_norm, ffn2_gate, ffn2_up, ffn2_down):
    b, s, d = x.shape
    assert d == D_MODEL and s % MIX_TOKENS == 0 and (b * s) % FFN_TOKENS == 0
    t = b * s
    row = lambda a: a.reshape(1, -1)

    x2d = x.reshape(t, d)
    for l in range(DEPTH):
        x2d = _ffn(x2d, row(ffn1_norm[l]), ffn1_gate[l].astype(BF16),
                   ffn1_up[l].astype(BF16), ffn1_down[l].astype(BF16))
        mix_in, q, k, v = _proj(
            x2d, row(mix_norm[l]), w_in[l].astype(BF16),
            row(jnp.tile(q_norm[l], N_HEADS)), row(jnp.tile(k_norm[l], N_HEADS)))
        table = _bias_table(rpb[l])
        x3d = _mix(
            x2d.reshape(b, s, d), mix_in.reshape(b, s, MIX_IN_W),
            q.reshape(b, s, ATTN_W), k.reshape(b, s, ATTN_W), v.reshape(b, s, ATTN_W),
            table, conv_dw[l].reshape(CONV_K, CONV_W // LANES, LANES).transpose(1, 0, 2),
            row(conv_dw_b[l]), row(conv_ln_g[l]), row(conv_ln_b[l]),
            conv_pw[l].astype(BF16), _block_diag(pool_w[l]).astype(BF16),
            row(pool_scale[l]), w_out[l].astype(BF16))
        x2d = _ffn(x3d.reshape(t, d), row(ffn2_norm[l]), ffn2_gate[l].astype(BF16),
                   ffn2_up[l].astype(BF16), ffn2_down[l].astype(BF16))
    return x2d.reshape(b, s, d)
```

```python
import functools

import numpy as np
import jax
import jax.numpy as jnp
from jax import lax
from jax.experimental import pallas as pl
from jax.experimental.pallas import tpu as pltpu

D_MODEL = 1024
DEPTH = 2
CONV_W = 256
POOL_W = 256
ATTN_W = 512
HEAD_DIM = 64
N_HEADS = 8
CONV_K = 31
POOL_WINDOWS = (2, 4, 8, 16)
POOL_GROUP = 64
GRID_W = 64
WIN_R = 8
WIN_C = 16
D_FF = 2816
IN_W = 2 * CONV_W + POOL_W + 3 * ATTN_W
MIX_IN_W = 2 * CONV_W + POOL_W
EPS = 1e-6
NEG = -1e30

F32 = jnp.float32
BF16 = jnp.bfloat16

VMEM_LIMIT_BYTES = 56 * 1024 * 1024

LANES = 128
SUBLANES = 8
BF16_SUBLANES = 16
FFN_TOKENS = 1024
MXU_TILE = 256
FF_CHUNKS = (6 * MXU_TILE, 5 * MXU_TILE)
PROJ_TOKENS = 512
MIX_ROWS = 8
MIX_TOKENS = MIX_ROWS * GRID_W
HALO = 16
ROW_CHUNK = GRID_W
assert sum(FF_CHUNKS) == D_FF and ROW_CHUNK >= max(POOL_WINDOWS) // 2
HEAD_GROUP = 4
GROUP_W = HEAD_GROUP * HEAD_DIM
N_GROUPS = N_HEADS // HEAD_GROUP
N_DR = 2 * WIN_R - 1
N_DR_PAIRS = N_DR - 1


def _const_spec(shape):
    nd = len(shape)
    return pl.BlockSpec(shape, lambda *_: (0,) * nd, pipeline_mode=pl.Buffered(1))


def _rms(x, g):
    return x * lax.rsqrt(jnp.mean(x * x, axis=-1, keepdims=True) + EPS) * g


def _ffn_compute(x, g_ref, wg_ref, wu_ref, wd_ref):
    h = _rms(x, g_ref[...]).astype(BF16)
    y = None
    start = 0
    for width in FF_CHUNKS:
        sl = slice(start, start + width)
        start += width
        gate = jnp.dot(h, wg_ref[:, sl], preferred_element_type=F32)
        up = jnp.dot(h, wu_ref[:, sl], preferred_element_type=F32)
        act = (gate * jax.nn.sigmoid(gate) * up).astype(BF16)
        part = jnp.dot(act, wd_ref[sl, :], preferred_element_type=F32)
        y = part if y is None else y + part
    return x + 0.5 * y


def _cast_plan(stacked, layer, steps):
    _, rows, cols = stacked.shape
    tiles = rows // BF16_SUBLANES
    assert tiles * BF16_SUBLANES == rows
    n_blk = max(d for d in range(1, min(steps, tiles) + 1) if tiles % d == 0)
    blk_rows = rows // n_blk
    block_of = lambda i: (i * n_blk) // steps
    return (pl.BlockSpec((1, blk_rows, cols), lambda i: (layer, block_of(i), 0)),
            pl.BlockSpec((blk_rows, cols), lambda i: (block_of(i), 0)),
            jax.ShapeDtypeStruct((rows, cols), BF16))


def _cast_blocks(src_refs, dst_refs):
    for src, dst in zip(src_refs, dst_refs):
        dst[...] = src[0].astype(BF16)


def _ffn_body(n_casts, x_ref, g_ref, wg_ref, wu_ref, wd_ref, *rest):
    o_ref = rest[n_casts]
    o_ref[...] = _ffn_compute(x_ref[...], g_ref, wg_ref, wu_ref, wd_ref)
    _cast_blocks(rest[:n_casts], rest[n_casts + 1:])


def _ffn(x2d, g, wg, wu, wd, casts=()):
    t = x2d.shape[0]
    steps = t // FFN_TOKENS
    plans = [_cast_plan(w, layer, steps) for w, layer in casts]
    out = pl.pallas_call(
        functools.partial(_ffn_body, len(casts)),
        grid=(steps,),
        in_specs=[
            pl.BlockSpec((FFN_TOKENS, D_MODEL), lambda i: (i, 0)),
            _const_spec((1, D_MODEL)),
            _const_spec((D_MODEL, D_FF)),
            _const_spec((D_MODEL, D_FF)),
            _const_spec((D_FF, D_MODEL)),
        ] + [p[0] for p in plans],
        out_specs=[pl.BlockSpec((FFN_TOKENS, D_MODEL), lambda i: (i, 0))] + [p[1] for p in plans],
        out_shape=[jax.ShapeDtypeStruct((t, D_MODEL), F32)] + [p[2] for p in plans],
        compiler_params=pltpu.CompilerParams(
            dimension_semantics=("arbitrary",), vmem_limit_bytes=VMEM_LIMIT_BYTES),
        name="ffn",
    )(x2d, g, wg, wu, wd, *[w for w, _ in casts])
    return out[0], out[1:]


def _head_norm(x, gain):
    first = lax.broadcasted_iota(jnp.int32, (1, LANES), 1) < HEAD_DIM
    tiles = []
    for v in range(x.shape[1] // LANES):
        xt = x[:, v * LANES:(v + 1) * LANES]
        sq = xt * xt
        ss0 = jnp.sum(jnp.where(first, sq, 0.0), axis=-1, keepdims=True)
        ss1 = jnp.sum(jnp.where(first, 0.0, sq), axis=-1, keepdims=True)
        inv = jnp.where(first, lax.rsqrt(ss0 * (1.0 / HEAD_DIM) + EPS),
                        lax.rsqrt(ss1 * (1.0 / HEAD_DIM) + EPS))
        tiles.append(xt * inv)
    return jnp.concatenate(tiles, axis=1) * gain


def _proj_body(n_casts, x_ref, g_ref, w_ref, qg_ref, kg_ref, *rest):
    mix_ref, q_ref, k_ref, v_ref = rest[n_casts:n_casts + 4]
    _cast_blocks(rest[:n_casts], rest[n_casts + 4:])
    h = _rms(x_ref[...], g_ref[...]).astype(BF16)
    u = jnp.dot(h, w_ref[...], preferred_element_type=F32)
    mix_ref[...] = u[:, :MIX_IN_W]
    q = u[:, MIX_IN_W:MIX_IN_W + ATTN_W]
    k = u[:, MIX_IN_W + ATTN_W:MIX_IN_W + 2 * ATTN_W]
    v = u[:, MIX_IN_W + 2 * ATTN_W:]
    q_ref[...] = (_head_norm(q, qg_ref[...]) * (HEAD_DIM ** -0.5)).astype(BF16)
    k_ref[...] = _head_norm(k, kg_ref[...]).astype(BF16)
    v_ref[...] = v.astype(BF16)


def _proj(x2d, g, w_in, q_gain, k_gain, casts=()):
    t = x2d.shape[0]
    steps = t // PROJ_TOKENS
    plans = [_cast_plan(w, layer, steps) for w, layer in casts]
    tok = lambda w: pl.BlockSpec((PROJ_TOKENS, w), lambda i: (i, 0))
    out = pl.pallas_call(
        functools.partial(_proj_body, len(casts)),
        grid=(steps,),
        in_specs=[
            tok(D_MODEL),
            _const_spec((1, D_MODEL)),
            _const_spec((D_MODEL, IN_W)),
            _const_spec((1, ATTN_W)),
            _const_spec((1, ATTN_W)),
        ] + [p[0] for p in plans],
        out_specs=[tok(MIX_IN_W), tok(ATTN_W), tok(ATTN_W), tok(ATTN_W)] + [p[1] for p in plans],
        out_shape=[
            jax.ShapeDtypeStruct((t, MIX_IN_W), F32),
            jax.ShapeDtypeStruct((t, ATTN_W), BF16),
            jax.ShapeDtypeStruct((t, ATTN_W), BF16),
            jax.ShapeDtypeStruct((t, ATTN_W), BF16),
        ] + [p[2] for p in plans],
        compiler_params=pltpu.CompilerParams(
            dimension_semantics=("arbitrary",), vmem_limit_bytes=VMEM_LIMIT_BYTES),
        name="proj_in",
    )(x2d, g, w_in, q_gain, k_gain, *[w for w, _ in casts])
    return out[:4], out[4:]


def _bias_table_body(rpb_ref, o_ref):
    g = pl.program_id(0)
    j = pl.program_id(1)
    qc = lax.broadcasted_iota(jnp.int32, (GRID_W, 2 * GRID_W), 0)
    lane = lax.broadcasted_iota(jnp.int32, (GRID_W, 2 * GRID_W), 1)
    kc = lane % GRID_W
    c0 = jnp.clip(qc - WIN_C // 2, 0, GRID_W - WIN_C)
    valid = (kc >= c0) & (kc < c0 + WIN_C)
    dc = kc - qc + (WIN_C - 1)
    second = lane >= GRID_W
    blocks = []
    for hh in range(HEAD_GROUP):
        head = g * HEAD_GROUP + hh
        t = jnp.zeros((GRID_W, 2 * GRID_W), F32)
        for c in range(2 * WIN_C - 1):
            val = jnp.where(second, rpb_ref[head, j + 1, c], rpb_ref[head, j, c])
            t = jnp.where(dc == c, val, t)
        blocks.append(jnp.where(valid, t, NEG))
    o_ref[0, 0] = jnp.concatenate(blocks, axis=0)


def _bias_table(rpb_l):
    return pl.pallas_call(
        _bias_table_body,
        grid=(N_GROUPS, N_DR_PAIRS),
        in_specs=[pl.BlockSpec(memory_space=pltpu.SMEM)],
        out_specs=pl.BlockSpec((1, 1, GROUP_W, 2 * GRID_W), lambda g, j: (g, j, 0, 0)),
        out_shape=jax.ShapeDtypeStruct((N_GROUPS, N_DR_PAIRS, GROUP_W, 2 * GRID_W), F32),
        compiler_params=pltpu.CompilerParams(dimension_semantics=("arbitrary", "arbitrary")),
        name="bias_table",
    )(rpb_l)


def _mix_body(x_ref, main_ref, left_ref, right_ref, q_ref, k_ref, v_ref, tab_ref,
              dw_ref, dwb_ref, lng_ref, lnb_ref, pw_ref, poolw_ref, pools_ref, wout_ref,
              o_ref, conv_ext, pool_ext, cat_ref):
    i = pl.program_id(1)
    n_blocks = pl.num_programs(1)
    seq = n_blocks * MIX_TOKENS

    def glu(blk):
        return blk[:, :CONV_W] * jax.nn.sigmoid(blk[:, CONV_W:2 * CONV_W])

    left = left_ref[0]
    right = right_ref[0]
    lmask = jnp.where(i > 0, 1.0, 0.0).astype(F32)
    rmask = jnp.where(i < n_blocks - 1, 1.0, 0.0).astype(F32)
    main = main_ref[0]
    conv_ext[0:HALO, :] = glu(left) * lmask
    conv_ext[HALO:HALO + MIX_TOKENS, :] = glu(main)
    conv_ext[HALO + MIX_TOKENS:, :] = glu(right) * rmask
    pool_ext[0:HALO, :] = left[:, 2 * CONV_W:] * lmask
    pool_ext[HALO:HALO + MIX_TOKENS, :] = main[:, 2 * CONV_W:]
    pool_ext[HALO + MIX_TOKENS:, :] = right[:, 2 * CONV_W:] * rmask

    lane128 = lax.broadcasted_iota(jnp.int32, (ROW_CHUNK, 128), 1)
    first_half = lane128 < POOL_GROUP
    dwb = dwb_ref[...]
    lng = lng_ref[...]
    lnb = lnb_ref[...]
    pools = pools_ref[...]
    pw = pw_ref[...]
    poolw = poolw_ref[...]

    def conv_taps(c, residues, y):
        base = c * ROW_CHUNK
        for b in residues:
            part = None
            for a in range(-(HALO // SUBLANES), HALO // SUBLANES):
                j = SUBLANES * a + b + CONV_K // 2
                if 0 <= j < CONV_K:
                    off = HALO + base + SUBLANES * a
                    tap = jnp.concatenate(
                        [dw_ref[v, pl.ds(j, ROW_CHUNK + SUBLANES, stride=0), :]
                         for v in range(CONV_W // LANES)], axis=1)
                    term = conv_ext[off:off + ROW_CHUNK + SUBLANES, :] * tap
                    part = term if part is None else part + term
            y = y + part[b:b + ROW_CHUNK, :]
        return y

    def conv_finish(c, y):
        base = c * ROW_CHUNK
        mu = jnp.mean(y, axis=-1, keepdims=True)
        yc = y - mu
        var = jnp.mean(yc * yc, axis=-1, keepdims=True)
        yn = yc * lax.rsqrt(var + EPS) * lng + lnb
        swish = (yn * jax.nn.sigmoid(yn)).astype(BF16)
        c_out = jnp.dot(swish, pw, preferred_element_type=F32)
        cat_ref[base:base + ROW_CHUNK, 0:CONV_W] = c_out.astype(BF16)

    def pool_chunk(c):
        base = c * ROW_CHUNK

        margin = max(POOL_WINDOWS) // 2
        rows = ROW_CHUNK + 2 * margin
        lo_row = HALO + base - margin

        def after(x, d):
            return pltpu.roll(x, (-d) % rows, 0)

        def core(x):
            return x[margin:margin + ROW_CHUNK, :]

        e01 = pool_ext[lo_row:lo_row + rows, 0:LANES]
        e23 = pool_ext[lo_row:lo_row + rows, LANES:2 * LANES]
        pair01 = e01 + after(e01, 1)
        a2 = core(e01 + after(e01, -1))
        a4 = core(pair01 + after(pair01, -2))
        pair23 = e23 + after(e23, 1)
        quad23 = pair23 + after(pair23, 2)
        a8 = core(quad23 + after(quad23, -4))
        oct23 = quad23 + after(quad23, 4)
        a16 = oct23[0:ROW_CHUNK, :] + core(oct23)

        clipped = c == 0 or c == MIX_TOKENS // ROW_CHUNK - 1
        tpos = i * MIX_TOKENS + base + lax.broadcasted_iota(jnp.int32, (ROW_CHUNK, 128), 0)

        def mean(total, w):
            if not clipped:
                return total * (1.0 / w)
            lo = jnp.maximum(tpos - w // 2, 0)
            hi = jnp.minimum(tpos - w // 2 + w, seq)
            return total / (hi - lo).astype(F32)

        mean01 =jnp.where(first_half, mean(a2, 2), mean(a4, 4))
        mean23 = jnp.where(first_half, mean(a8, 8), mean(a16, 16))
        tok = pool_ext[HALO + base:HALO + base + ROW_CHUNK, :]
        mixed = (jnp.concatenate([mean01, mean23], axis=1) - tok).astype(BF16)
        p_out = jnp.dot(mixed, poolw, preferred_element_type=F32) * pools
        cat_ref[base:base + ROW_CHUNK, CONV_W:CONV_W + POOL_W] = p_out.astype(BF16)

    n_rows = n_blocks * MIX_ROWS
    lane_head = lax.broadcasted_iota(jnp.int32, (GRID_W, GROUP_W), 1) // HEAD_DIM

    def window(r):
        row = i * MIX_ROWS + r
        r0 = jnp.clip(row - WIN_R // 2, 0, n_rows - WIN_R)
        return pl.multiple_of(r0 * GRID_W, GRID_W), r0 - row + (WIN_R - 1)

    def att_scores(r, g):
        kstart, j0 = window(r)
        cols = slice(g * GROUP_W, (g + 1) * GROUP_W)
        q_r = q_ref[0, r * GRID_W:(r + 1) * GRID_W, cols]
        kwin = k_ref[0, pl.ds(kstart, WIN_R * GRID_W), cols]
        zero = jnp.zeros_like(q_r)
        q_bd = jnp.concatenate(
            [jnp.where(lane_head == hh, q_r, zero) for hh in range(HEAD_GROUP)], axis=0)
        return lax.dot_general(q_bd, kwin, (((1,), (1,)), ((), ())),
                               preferred_element_type=F32)

    def att_softmax(r, g, sc):
        _, j0 = window(r)
        bias = jnp.concatenate(
            [tab_ref[g, j0 + 2 * m] for m in range(WIN_R // 2)], axis=1)
        sc = sc + bias
        mx = jnp.max(sc, axis=-1, keepdims=True)
        p = jnp.exp(sc - mx)
        return p.astype(BF16), jnp.sum(p, axis=-1, keepdims=True)

    def att_values(r, g, p, den):
        kstart, _ = window(r)
        cols = slice(g * GROUP_W, (g + 1) * GROUP_W)
        vwin = v_ref[0, pl.ds(kstart, WIN_R * GRID_W), cols]
        return jnp.dot(p, vwin, preferred_element_type=F32) / den

    def att_finish(r, g, o_all):
        out = jnp.zeros((GRID_W, GROUP_W), F32)
        for hh in range(HEAD_GROUP):
            out = jnp.where(lane_head == hh, o_all[hh * GRID_W:(hh + 1) * GRID_W, :], out)
        cat_ref[r * GRID_W:(r + 1) * GRID_W,
                CONV_W + POOL_W + g * GROUP_W:CONV_W + POOL_W + (g + 1) * GROUP_W] = (
                    out.astype(BF16))

    for c in range(MIX_TOKENS // ROW_CHUNK):
        conv_finish(c, conv_taps(c, range(SUBLANES), dwb))
        pool_chunk(c)
    for r in range(MIX_ROWS):
        for g in range(N_GROUPS):
            p, den = att_softmax(r, g, att_scores(r, g))
            att_finish(r, g, att_values(r, g, p, den))

    o_ref[0] = x_ref[0] + jnp.dot(cat_ref[...], wout_ref[...], preferred_element_type=F32)


def _mix(x3d, mix3d, q3d, k3d, v3d, table, dw, dwb, lng, lnb, pw, poolw, pools, wout):
    b, s, _ = x3d.shape
    n_blocks = s // MIX_TOKENS
    per_blk = MIX_TOKENS // HALO
    n_halo = s // HALO
    blk = lambda w: pl.BlockSpec((1, MIX_TOKENS, w), lambda bi, i: (bi, i, 0))
    full = lambda w: pl.BlockSpec((1, s, w), lambda bi, i: (bi, 0, 0))
    return pl.pallas_call(
        _mix_body,
        grid=(b, n_blocks),
        in_specs=[
            blk(D_MODEL),
            blk(MIX_IN_W),
            pl.BlockSpec((1, HALO, MIX_IN_W),
                         lambda bi, i: (bi, jnp.maximum(i * per_blk - 1, 0), 0)),
            pl.BlockSpec((1, HALO, MIX_IN_W),
                         lambda bi, i: (bi, jnp.minimum((i + 1) * per_blk, n_halo - 1), 0)),
            blk(ATTN_W),
            full(ATTN_W),
            full(ATTN_W),
            _const_spec(table.shape),
            _const_spec(dw.shape),
            _const_spec(dwb.shape),
            _const_spec(lng.shape),
            _const_spec(lnb.shape),
            _const_spec(pw.shape),
            _const_spec(poolw.shape),
            _const_spec(pools.shape),
            _const_spec(wout.shape),
        ],
        out_specs=blk(D_MODEL),
        out_shape=jax.ShapeDtypeStruct((b, s, D_MODEL), F32),
        scratch_shapes=[
            pltpu.VMEM((MIX_TOKENS + 2 * HALO, CONV_W), F32),
            pltpu.VMEM((MIX_TOKENS + 2 * HALO, POOL_W), F32),
            pltpu.VMEM((MIX_TOKENS, D_MODEL), BF16),
        ],
        compiler_params=pltpu.CompilerParams(
            dimension_semantics=("arbitrary", "arbitrary"), vmem_limit_bytes=VMEM_LIMIT_BYTES),
        name="mixer",
    )(x3d, mix3d, mix3d, mix3d, q3d, k3d, v3d, table, dw, dwb, lng, lnb, pw, poolw, pools, wout)


def _block_diag(blocks):
    n, r, c = blocks.shape
    eye = jnp.eye(n, dtype=blocks.dtype)
    return (eye[:, None, :, None] * blocks[:, :, None, :]).reshape(n * r, n * c)


def kernel(x, ffn1_norm, ffn1_gate, ffn1_up, ffn1_down, mix_norm, w_in, conv_dw, conv_dw_b,
           conv_ln_g, conv_ln_b, conv_pw, pool_w, pool_scale, q_norm, k_norm, rpb, w_out,
           ffn2_norm, ffn2_gate, ffn2_up, ffn2_down):
    b, s, d = x.shape
    assert d == D_MODEL and s % MIX_TOKENS == 0 and (b * s) % FFN_TOKENS == 0
    t = b * s
    row = lambda a: a.reshape(1, -1)

    x2d = x.reshape(t, d)
    ffn1_w = [w[0].astype(BF16) for w in (ffn1_gate, ffn1_up, ffn1_down)]
    for l in range(DEPTH):
        x2d, (gate2, up2, down2, w_in_l) = _ffn(
            x2d, row(ffn1_norm[l]), *ffn1_w,
            casts=((ffn2_gate, l), (ffn2_up, l), (ffn2_down, l), (w_in, l)))
        (mix_in, q, k, v), (w_out_l, conv_pw_l) = _proj(
            x2d, row(mix_norm[l]), w_in_l,
            row(jnp.tile(q_norm[l], N_HEADS)), row(jnp.tile(k_norm[l], N_HEADS)),
            casts=((w_out, l), (conv_pw, l)))
        table = _bias_table(rpb[l])
        x3d = _mix(
            x2d.reshape(b, s, d), mix_in.reshape(b, s, MIX_IN_W),
            q.reshape(b, s, ATTN_W), k.reshape(b, s, ATTN_W), v.reshape(b, s, ATTN_W),
            table, conv_dw[l].reshape(CONV_K, CONV_W // LANES, LANES).transpose(1, 0, 2),
            row(conv_dw_b[l]), row(conv_ln_g[l]), row(conv_ln_b[l]),
            conv_pw_l, _block_diag(pool_w[l]).astype(BF16),
            row(pool_scale[l]), w_out_l)
        next_ffn1 = (() if l + 1 == DEPTH else
                     ((ffn1_gate, l + 1), (ffn1_up, l + 1), (ffn1_down, l + 1)))
        x2d, ffn1_w = _ffn(x3d.reshape(t, d), row(ffn2_norm[l]), gate2, up2, down2,
                           casts=next_ffn1)
    return x2d.reshape(b, s, d)
```

```python
import functools

import numpy as np
import jax
import jax.numpy as jnp
from jax import lax
from jax.experimental import pallas as pl
from jax.experimental.pallas import tpu as pltpu

D_MODEL = 1024
DEPTH = 2
CONV_W = 256
POOL_W = 256
ATTN_W = 512
HEAD_DIM = 64
N_HEADS = 8
CONV_K = 31
POOL_WINDOWS = (2, 4, 8, 16)
POOL_GROUP = 64
GRID_W = 64
WIN_R = 8
WIN_C = 16
D_FF = 2816
IN_W = 2 * CONV_W + POOL_W + 3 * ATTN_W
MIX_IN_W = 2 * CONV_W + POOL_W
EPS = 1e-6
NEG = -1e30

F32 = jnp.float32
BF16 = jnp.bfloat16

VMEM_LIMIT_BYTES = 56 * 1024 * 1024

LANES = 128
SUBLANES = 8
BF16_SUBLANES = 16
FFN_TOKENS = 1024
MXU_TILE = 256
FF_CHUNKS = (6 * MXU_TILE, 5 * MXU_TILE)
PROJ_TOKENS = 1024
MIX_ROWS = 8
MIX_TOKENS = MIX_ROWS * GRID_W
HALO = 16
ROW_CHUNK = GRID_W
assert sum(FF_CHUNKS) == D_FF and ROW_CHUNK >= max(POOL_WINDOWS) // 2
HEAD_GROUP = 4
GROUP_W = HEAD_GROUP * HEAD_DIM
N_GROUPS = N_HEADS // HEAD_GROUP
N_DR = 2 * WIN_R - 1
N_DR_PAIRS = N_DR - 1


def _const_spec(shape):
    nd = len(shape)
    return pl.BlockSpec(shape, lambda *_: (0,) * nd, pipeline_mode=pl.Buffered(1))


def _inv_rms(x):
    return lax.rsqrt(jnp.mean(x * x, axis=-1, keepdims=True) + EPS)


def _ffn_compute(x, g_ref, wg_ref, wu_ref, wd_ref):
    inv = _inv_rms(x)
    h = (x * g_ref[...]).astype(BF16)
    y = None
    start = 0
    for width in FF_CHUNKS:
        sl = slice(start, start + width)
        start += width
        gate = jnp.dot(h, wg_ref[:, sl], preferred_element_type=F32) * inv
        up = jnp.dot(h, wu_ref[:, sl], preferred_element_type=F32) * inv
        act = (gate * jax.nn.sigmoid(gate) * up).astype(BF16)
        part = jnp.dot(act, wd_ref[sl, :], preferred_element_type=F32)
        y = part if y is None else y + part
    return x + 0.5 * y


def _cast_plan(stacked, layer, steps):
    _, rows, cols = stacked.shape
    tiles = rows // BF16_SUBLANES
    assert tiles * BF16_SUBLANES == rows
    n_blk = max(d for d in range(1, min(steps, tiles) + 1) if tiles % d == 0)
    blk_rows = rows // n_blk
    block_of = lambda i: (i * n_blk) // steps
    return (pl.BlockSpec((1, blk_rows, cols), lambda i: (layer, block_of(i), 0)),
            pl.BlockSpec((blk_rows, cols), lambda i: (block_of(i), 0)),
            jax.ShapeDtypeStruct((rows, cols), BF16))


def _cast_blocks(src_refs, dst_refs):
    for src, dst in zip(src_refs, dst_refs):
        dst[...] = src[0].astype(BF16)


def _ffn_body(n_casts, x_ref, g_ref, wg_ref, wu_ref, wd_ref, *rest):
    o_ref = rest[n_casts]
    o_ref[...] = _ffn_compute(x_ref[...], g_ref, wg_ref, wu_ref, wd_ref)
    _cast_blocks(rest[:n_casts], rest[n_casts + 1:])


def _ffn(x2d, g, wg, wu, wd, casts=()):
    t = x2d.shape[0]
    steps = t // FFN_TOKENS
    plans = [_cast_plan(w, layer, steps) for w, layer in casts]
    out = pl.pallas_call(
        functools.partial(_ffn_body, len(casts)),
        grid=(steps,),
        in_specs=[
            pl.BlockSpec((FFN_TOKENS, D_MODEL), lambda i: (i, 0)),
            _const_spec((1, D_MODEL)),
            _const_spec((D_MODEL, D_FF)),
            _const_spec((D_MODEL, D_FF)),
            _const_spec((D_FF, D_MODEL)),
        ] + [p[0] for p in plans],
        out_specs=[pl.BlockSpec((FFN_TOKENS, D_MODEL), lambda i: (i, 0))] + [p[1] for p in plans],
        out_shape=[jax.ShapeDtypeStruct((t, D_MODEL), F32)] + [p[2] for p in plans],
        compiler_params=pltpu.CompilerParams(
            dimension_semantics=("arbitrary",), vmem_limit_bytes=VMEM_LIMIT_BYTES),
        name="ffn",
    )(x2d, g, wg, wu, wd, *[w for w, _ in casts])
    return out[0], out[1:]


def _head_norm(x, gain):
    first = lax.broadcasted_iota(jnp.int32, (1, LANES), 1) < HEAD_DIM
    tiles = []
    for v in range(x.shape[1] // LANES):
        xt = x[:, v * LANES:(v + 1) * LANES]
        sq = xt * xt
        ss0 = jnp.sum(jnp.where(first, sq, 0.0), axis=-1, keepdims=True)
        ss1 = jnp.sum(jnp.where(first, 0.0, sq), axis=-1, keepdims=True)
        inv = jnp.where(first, lax.rsqrt(ss0 * (1.0 / HEAD_DIM) + EPS),
                        lax.rsqrt(ss1 * (1.0 / HEAD_DIM) + EPS))
        tiles.append(xt * inv)
    return jnp.concatenate(tiles, axis=1) * gain


def _proj_body(n_casts, x_ref, g_ref, w_ref, qg_ref, kg_ref, *rest):
    mix_ref, q_ref, k_ref, v_ref = rest[n_casts:n_casts + 4]
    _cast_blocks(rest[:n_casts], rest[n_casts + 4:])
    x = x_ref[...]
    h = (x * g_ref[...]).astype(BF16)
    u = jnp.dot(h, w_ref[...], preferred_element_type=F32) * _inv_rms(x)
    mix_ref[...] = u[:, :MIX_IN_W]
    q = u[:, MIX_IN_W:MIX_IN_W + ATTN_W]
    k = u[:, MIX_IN_W + ATTN_W:MIX_IN_W + 2 * ATTN_W]
    v = u[:, MIX_IN_W + 2 * ATTN_W:]
    q_ref[...] = (_head_norm(q, qg_ref[...]) * (HEAD_DIM ** -0.5)).astype(BF16)
    k_ref[...] = _head_norm(k, kg_ref[...]).astype(BF16)
    v_ref[...] = v.astype(BF16)


def _proj(x2d, g, w_in, q_gain, k_gain, casts=()):
    t = x2d.shape[0]
    steps = t // PROJ_TOKENS
    plans = [_cast_plan(w, layer, steps) for w, layer in casts]
    tok = lambda w: pl.BlockSpec((PROJ_TOKENS, w), lambda i: (i, 0))
    out = pl.pallas_call(
        functools.partial(_proj_body, len(casts)),
        grid=(steps,),
        in_specs=[
            tok(D_MODEL),
            _const_spec((1, D_MODEL)),
            _const_spec((D_MODEL, IN_W)),
            _const_spec((1, ATTN_W)),
            _const_spec((1, ATTN_W)),
        ] + [p[0] for p in plans],
        out_specs=[tok(MIX_IN_W), tok(ATTN_W), tok(ATTN_W), tok(ATTN_W)] + [p[1] for p in plans],
        out_shape=[
            jax.ShapeDtypeStruct((t, MIX_IN_W), F32),
            jax.ShapeDtypeStruct((t, ATTN_W), BF16),
            jax.ShapeDtypeStruct((t, ATTN_W), BF16),
            jax.ShapeDtypeStruct((t, ATTN_W), BF16),
        ] + [p[2] for p in plans],
        compiler_params=pltpu.CompilerParams(
            dimension_semantics=("arbitrary",), vmem_limit_bytes=VMEM_LIMIT_BYTES),
        name="proj_in",
    )(x2d, g, w_in, q_gain, k_gain, *[w for w, _ in casts])
    return out[:4], out[4:]


def _bias_table_body(rpb_ref, o_ref):
    g = pl.program_id(0)
    qc = lax.broadcasted_iota(jnp.int32, (GRID_W, LANES), 0)
    lane = lax.broadcasted_iota(jnp.int32, (GRID_W, LANES), 1)
    kc = lane % GRID_W
    c0 = jnp.clip(qc - WIN_C // 2, 0, GRID_W - WIN_C)
    valid = (kc >= c0) & (kc < c0 + WIN_C)
    first = lane < GRID_W
    qc_all = lax.broadcasted_iota(jnp.int32, (N_DR * GRID_W, LANES), 0) % GRID_W
    for hh in range(HEAD_GROUP):
        head = g * HEAD_GROUP + hh
        t = jnp.concatenate(
            [jnp.broadcast_to(rpb_ref[head, d:d + 1, :], (GRID_W, LANES)) for d in range(N_DR)],
            axis=0)
        t = pltpu.roll(t, LANES - (WIN_C - 1), 1)
        for bit in range(GRID_W.bit_length() - 1):
            t = jnp.where(((qc_all >> bit) & 1) == 1, pltpu.roll(t, 1 << bit, 1), t)
        shifted = pltpu.roll(t, GRID_W, 1)
        for j in range(N_DR_PAIRS):
            pair = jnp.where(first, t[j * GRID_W:(j + 1) * GRID_W, :],
                             shifted[(j + 1) * GRID_W:(j + 2) * GRID_W, :])
            o_ref[0, j, hh * GRID_W:(hh + 1) * GRID_W, :] = jnp.where(valid, pair, NEG)


def _bias_table(rpb_l):
    padded = jnp.pad(rpb_l, ((0, 0), (0, 0), (0, LANES - rpb_l.shape[-1])))
    return pl.pallas_call(
        _bias_table_body,
        grid=(N_GROUPS,),
        in_specs=[_const_spec(padded.shape)],
        out_specs=pl.BlockSpec((1, N_DR_PAIRS, GROUP_W, 2 * GRID_W), lambda g: (g, 0, 0, 0)),
        out_shape=jax.ShapeDtypeStruct((N_GROUPS, N_DR_PAIRS, GROUP_W, 2 * GRID_W), F32),
        compiler_params=pltpu.CompilerParams(dimension_semantics=("arbitrary",)),
        name="bias_table",
    )(padded)


def _mix_body(x_ref, main_ref, left_ref, right_ref, q_ref, k_ref, v_ref, tab_ref,
              dw_ref, dwb_ref, lng_ref, lnb_ref, pw_ref, poolw_ref, pools_ref, wout_ref,
              o_ref, conv_ext, pool_ext, cat_ref):
    i = pl.program_id(1)
    n_blocks = pl.num_programs(1)
    seq = n_blocks * MIX_TOKENS

    def glu(blk):
        return blk[:, :CONV_W] * jax.nn.sigmoid(blk[:, CONV_W:2 * CONV_W])

    left = left_ref[0]
    right = right_ref[0]
    lmask = jnp.where(i > 0, 1.0, 0.0).astype(F32)
    rmask = jnp.where(i < n_blocks - 1, 1.0, 0.0).astype(F32)
    main = main_ref[0]
    conv_ext[0:HALO, :] = glu(left) * lmask
    conv_ext[HALO:HALO + MIX_TOKENS, :] = glu(main)
    conv_ext[HALO + MIX_TOKENS:, :] = glu(right) * rmask
    pool_ext[0:HALO, :] = left[:, 2 * CONV_W:] * lmask
    pool_ext[HALO:HALO + MIX_TOKENS, :] = main[:, 2 * CONV_W:]
    pool_ext[HALO + MIX_TOKENS:, :] = right[:, 2 * CONV_W:] * rmask

    lane128 = lax.broadcasted_iota(jnp.int32, (ROW_CHUNK, 128), 1)
    first_half = lane128 < POOL_GROUP
    dwb = dwb_ref[...]
    lng = lng_ref[...]
    lnb = lnb_ref[...]
    pools = pools_ref[...]
    pw = pw_ref[...]
    poolw = poolw_ref[...]

    def conv_taps(c, residues, y):
        base = c * ROW_CHUNK
        for b in residues:
            part = None
            for a in range(-(HALO // SUBLANES), HALO // SUBLANES):
                j = SUBLANES * a + b + CONV_K // 2
                if 0 <= j < CONV_K:
                    off = HALO + base + SUBLANES * a
                    tap = jnp.concatenate(
                        [dw_ref[v, pl.ds(j, ROW_CHUNK + SUBLANES, stride=0), :]
                         for v in range(CONV_W // LANES)], axis=1)
                    term = conv_ext[off:off + ROW_CHUNK + SUBLANES, :] * tap
                    part = term if part is None else part + term
            y = y + part[b:b + ROW_CHUNK, :]
        return y

    def conv_finish(c, y):
        base = c * ROW_CHUNK
        mu = jnp.mean(y, axis=-1, keepdims=True)
        yc = y - mu
        var = jnp.mean(yc * yc, axis=-1, keepdims=True)
        yn = yc * lax.rsqrt(var + EPS) * lng + lnb
        swish = (yn * jax.nn.sigmoid(yn)).astype(BF16)
        c_out = jnp.dot(swish, pw, preferred_element_type=F32)
        cat_ref[base:base + ROW_CHUNK, 0:CONV_W] = c_out.astype(BF16)

    def pool_chunk(c):
        base = c * ROW_CHUNK

        margin = max(POOL_WINDOWS) // 2
        rows = ROW_CHUNK + 2 * margin
        lo_row = HALO + base - margin

        def after(x, d):
            return pltpu.roll(x, (-d) % rows, 0)

        def core(x):
            return x[margin:margin + ROW_CHUNK, :]

        e01 = pool_ext[lo_row:lo_row + rows, 0:LANES]
        e23 = pool_ext[lo_row:lo_row + rows, LANES:2 * LANES]
        pair01 = e01 + after(e01, 1)
        a2 = core(e01 + after(e01, -1))
        a4 = core(pair01 + after(pair01, -2))
        pair23 = e23 + after(e23, 1)
        quad23 = pair23 + after(pair23, 2)
        a8 = core(quad23 + after(quad23, -4))
        oct23 = quad23 + after(quad23, 4)
        a16 = oct23[0:ROW_CHUNK, :] + core(oct23)

        clipped = c == 0 or c == MIX_TOKENS // ROW_CHUNK - 1
        tpos = i * MIX_TOKENS + base + lax.broadcasted_iota(jnp.int32, (ROW_CHUNK, 128), 0)

        def mean(total, w):
            if not clipped:
                return total * (1.0 / w)
            lo = jnp.maximum(tpos - w // 2, 0)
            hi = jnp.minimum(tpos - w // 2 + w, seq)
            return total / (hi - lo).astype(F32)

        mean01 =jnp.where(first_half, mean(a2, 2), mean(a4, 4))
        mean23 = jnp.where(first_half, mean(a8, 8), mean(a16, 16))
        tok = pool_ext[HALO + base:HALO + base + ROW_CHUNK, :]
        mixed = (jnp.concatenate([mean01, mean23], axis=1) - tok).astype(BF16)
        p_out = jnp.dot(mixed, poolw, preferred_element_type=F32) * pools
        cat_ref[base:base + ROW_CHUNK, CONV_W:CONV_W + POOL_W] = p_out.astype(BF16)

    n_rows = n_blocks * MIX_ROWS
    lane_head = lax.broadcasted_iota(jnp.int32, (GRID_W, GROUP_W), 1) // HEAD_DIM

    def window(r):
        row = i * MIX_ROWS + r
        r0 = jnp.clip(row - WIN_R // 2, 0, n_rows - WIN_R)
        return pl.multiple_of(r0 * GRID_W, GRID_W), r0 - row + (WIN_R - 1)

    def att_scores(r, g):
        kstart, j0 = window(r)
        cols = slice(g * GROUP_W, (g + 1) * GROUP_W)
        q_r = q_ref[0, r * GRID_W:(r + 1) * GRID_W, cols]
        kwin = k_ref[0, pl.ds(kstart, WIN_R * GRID_W), cols]
        zero = jnp.zeros_like(q_r)
        q_bd = jnp.concatenate(
            [jnp.where(lane_head == hh, q_r, zero) for hh in range(HEAD_GROUP)], axis=0)
        return lax.dot_general(q_bd, kwin, (((1,), (1,)), ((), ())),
                               preferred_element_type=F32)

    def att_softmax(r, g, sc):
        _, j0 = window(r)
        bias = jnp.concatenate(
            [tab_ref[g, j0 + 2 * m] for m in range(WIN_R // 2)], axis=1)
        sc = sc + bias
        mx = jnp.max(sc, axis=-1, keepdims=True)
        p = jnp.exp(sc - mx)
        return p.astype(BF16), jnp.sum(p, axis=-1, keepdims=True)

    def att_values(r, g, p, den):
        kstart, _ = window(r)
        cols = slice(g * GROUP_W, (g + 1) * GROUP_W)
        vwin = v_ref[0, pl.ds(kstart, WIN_R * GRID_W), cols]
        return jnp.dot(p, vwin, preferred_element_type=F32) / den

    def att_finish(r, g, o_all):
        out = jnp.zeros((GRID_W, GROUP_W), F32)
        for hh in range(HEAD_GROUP):
            out = jnp.where(lane_head == hh, o_all[hh * GRID_W:(hh + 1) * GRID_W, :], out)
        cat_ref[r * GRID_W:(r + 1) * GRID_W,
                CONV_W + POOL_W + g * GROUP_W:CONV_W + POOL_W + (g + 1) * GROUP_W] = (
                    out.astype(BF16))

    for c in range(MIX_TOKENS // ROW_CHUNK):
        conv_finish(c, conv_taps(c, range(SUBLANES), dwb))
        pool_chunk(c)
    for r in range(MIX_ROWS):
        for g in range(N_GROUPS):
            p, den = att_softmax(r, g, att_scores(r, g))
            att_finish(r, g, att_values(r, g, p, den))

    o_ref[0] = x_ref[0] + jnp.dot(cat_ref[...], wout_ref[...], preferred_element_type=F32)


def _mix(x3d, mix3d, q3d, k3d, v3d, table, dw, dwb, lng, lnb, pw, poolw, pools, wout):
    b, s, _ = x3d.shape
    n_blocks = s // MIX_TOKENS
    per_blk = MIX_TOKENS // HALO
    n_halo = s // HALO
    blk = lambda w: pl.BlockSpec((1, MIX_TOKENS, w), lambda bi, i: (bi, i, 0))
    full = lambda w: pl.BlockSpec((1, s, w), lambda bi, i: (bi, 0, 0))
    return pl.pallas_call(
        _mix_body,
        grid=(b, n_blocks),
        in_specs=[
            blk(D_MODEL),
            blk(MIX_IN_W),
            pl.BlockSpec((1, HALO, MIX_IN_W),
                         lambda bi, i: (bi, jnp.maximum(i * per_blk - 1, 0), 0)),
            pl.BlockSpec((1, HALO, MIX_IN_W),
                         lambda bi, i: (bi, jnp.minimum((i + 1) * per_blk, n_halo - 1), 0)),
            blk(ATTN_W),
            full(ATTN_W),
            full(ATTN_W),
            _const_spec(table.shape),
            _const_spec(dw.shape),
            _const_spec(dwb.shape),
            _const_spec(lng.shape),
            _const_spec(lnb.shape),
            _const_spec(pw.shape),
            _const_spec(poolw.shape),
            _const_spec(pools.shape),
            _const_spec(wout.shape),
        ],
        out_specs=blk(D_MODEL),
        out_shape=jax.ShapeDtypeStruct((b, s, D_MODEL), F32),
        scratch_shapes=[
            pltpu.VMEM((MIX_TOKENS + 2 * HALO, CONV_W), F32),
            pltpu.VMEM((MIX_TOKENS + 2 * HALO, POOL_W), F32),
            pltpu.VMEM((MIX_TOKENS, D_MODEL), BF16),
        ],
        compiler_params=pltpu.CompilerParams(
            dimension_semantics=("arbitrary", "arbitrary"), vmem_limit_bytes=VMEM_LIMIT_BYTES),
        name="mixer",
    )(x3d, mix3d, mix3d, mix3d, q3d, k3d, v3d, table, dw, dwb, lng, lnb, pw, poolw, pools, wout)


def _block_diag(blocks):
    n, r, c = blocks.shape
    eye = jnp.eye(n, dtype=blocks.dtype)
    return (eye[:, None, :, None] * blocks[:, :, None, :]).reshape(n * r, n * c)


def kernel(x, ffn1_norm, ffn1_gate, ffn1_up, ffn1_down, mix_norm, w_in, conv_dw, conv_dw_b,
           conv_ln_g, conv_ln_b, conv_pw, pool_w, pool_scale, q_norm, k_norm, rpb, w_out,
           ffn2_norm, ffn2_gate, ffn2_up, ffn2_down):
    b, s, d = x.shape
    assert d == D_MODEL and s % MIX_TOKENS == 0 and (b * s) % FFN_TOKENS == 0
    t = b * s
    row = lambda a: a.reshape(1, -1)

    x2d = x.reshape(t, d)
    ffn1_w = [w[0].astype(BF16) for w in (ffn1_gate, ffn1_up, ffn1_down)]
    for l in range(DEPTH):
        x2d, (gate2, up2, down2, w_in_l) = _ffn(
            x2d, row(ffn1_norm[l]), *ffn1_w,
            casts=((ffn2_gate, l), (ffn2_up, l), (ffn2_down, l), (w_in, l)))
        (mix_in, q, k, v), (w_out_l, conv_pw_l) = _proj(
            x2d, row(mix_norm[l]), w_in_l,
            row(jnp.tile(q_norm[l], N_HEADS)), row(jnp.tile(k_norm[l], N_HEADS)),
            casts=((w_out, l), (conv_pw, l)))
        table = _bias_table(rpb[l])
        x3d = _mix(
            x2d.reshape(b, s, d), mix_in.reshape(b, s, MIX_IN_W),
            q.reshape(b, s, ATTN_W), k.reshape(b, s, ATTN_W), v.reshape(b, s, ATTN_W),
            table, conv_dw[l].reshape(CONV_K, CONV_W // LANES, LANES).transpose(1, 0, 2),
            row(conv_dw_b[l]), row(conv_ln_g[l]), row(conv_ln_b[l]),
            conv_pw_l, _block_diag(pool_w[l]).astype(BF16),
            row(pool_scale[l]), w_out_l)
        next_ffn1 = (() if l + 1 == DEPTH else
                     ((ffn1_gate, l + 1), (ffn1_up, l + 1), (ffn1_down, l + 1)))
        x2d, ffn1_w = _ffn(x3d.reshape(t, d), row(ffn2_norm[l]), gate2, up2, down2,
                           casts=next_ffn1)
    return x2d.reshape(b, s, d)
```

```python
import functools

import jax
import jax.numpy as jnp
from jax import lax
from jax.experimental import pallas as pl
from jax.experimental.pallas import tpu as pltpu

D_MODEL = 1024
DEPTH = 2
CONV_W = 256
POOL_W = 256
ATTN_W = 512
HEAD_DIM = 64
N_HEADS = 8
CONV_K = 31
POOL_WINDOWS = (2, 4, 8, 16)
POOL_GROUP = 64
GRID_W = 64
WIN_R = 8
WIN_C = 16
D_FF = 2816
IN_W = 2 * CONV_W + POOL_W + 3 * ATTN_W
MIX_IN_W = 2 * CONV_W + POOL_W
MIX_W = CONV_W + POOL_W
EPS = 1e-6
NEG = -1e30

F32 = jnp.float32
BF16 = jnp.bfloat16

VMEM_LIMIT_BYTES = 56 * 1024 * 1024

LANES = 128
SUBLANES = 8
BF16_SUBLANES = 16
FFN_TOKENS = 1024
MXU_TILE = 256
FF_CHUNKS = (6 * MXU_TILE, 5 * MXU_TILE)
PROJ_TOKENS = 1024
MIX_ROWS = 8
MIX_TOKENS = MIX_ROWS * GRID_W
HALO = 16
ROW_CHUNK = 2 * GRID_W
assert sum(FF_CHUNKS) == D_FF and ROW_CHUNK >= max(POOL_WINDOWS) // 2
HEAD_GROUP = 4
GROUP_W = HEAD_GROUP * HEAD_DIM
N_GROUPS = N_HEADS // HEAD_GROUP
N_DR = 2 * WIN_R - 1
N_DR_PAIRS = N_DR - 1


def _const_spec(shape):
    nd = len(shape)
    return pl.BlockSpec(shape, lambda *_: (0,) * nd, pipeline_mode=pl.Buffered(1))


def _inv_rms(x):
    return lax.rsqrt(jnp.mean(x * x, axis=-1, keepdims=True) + EPS)


def _ffn_compute(x, g_ref, wg_ref, wu_ref, wd_ref):
    inv = _inv_rms(x)
    h = (x * g_ref[...]).astype(BF16)
    y = None
    start = 0
    for width in FF_CHUNKS:
        sl = slice(start, start + width)
        start += width
        gate = jnp.dot(h, wg_ref[:, sl], preferred_element_type=F32) * inv
        up = jnp.dot(h, wu_ref[:, sl], preferred_element_type=F32) * inv
        act = (gate * jax.nn.sigmoid(gate) * up).astype(BF16)
        part = jnp.dot(act, wd_ref[sl, :], preferred_element_type=F32)
        y = part if y is None else y + part
    return x + 0.5 * y


def _cast_plan(stacked, layer, steps):
    _, rows, cols = stacked.shape
    tiles = rows // BF16_SUBLANES
    assert tiles * BF16_SUBLANES == rows
    n_blk = max(d for d in range(1, min(steps, tiles) + 1) if tiles % d == 0)
    blk_rows = rows // n_blk
    block_of = lambda i: (i * n_blk) // steps
    return (pl.BlockSpec((1, blk_rows, cols), lambda i: (layer, block_of(i), 0)),
            pl.BlockSpec((blk_rows, cols), lambda i: (block_of(i), 0)),
            jax.ShapeDtypeStruct((rows, cols), BF16))


def _cast_blocks(src_refs, dst_refs):
    for src, dst in zip(src_refs, dst_refs):
        dst[...] = src[0].astype(BF16)


def _ffn_body(n_casts, x_ref, g_ref, wg_ref, wu_ref, wd_ref, *rest):
    o_ref = rest[n_casts]
    o_ref[...] = _ffn_compute(x_ref[...], g_ref, wg_ref, wu_ref, wd_ref)
    _cast_blocks(rest[:n_casts], rest[n_casts + 1:])


def _ffn(x2d, g, wg, wu, wd, casts=()):
    t = x2d.shape[0]
    steps = t // FFN_TOKENS
    plans = [_cast_plan(w, layer, steps) for w, layer in casts]
    out = pl.pallas_call(
        functools.partial(_ffn_body, len(casts)),
        grid=(steps,),
        in_specs=[
            pl.BlockSpec((FFN_TOKENS, D_MODEL), lambda i: (i, 0)),
            _const_spec((1, D_MODEL)),
            _const_spec((D_MODEL, D_FF)),
            _const_spec((D_MODEL, D_FF)),
            _const_spec((D_FF, D_MODEL)),
        ] + [p[0] for p in plans],
        out_specs=[pl.BlockSpec((FFN_TOKENS, D_MODEL), lambda i: (i, 0))] + [p[1] for p in plans],
        out_shape=[jax.ShapeDtypeStruct((t, D_MODEL), F32)] + [p[2] for p in plans],
        compiler_params=pltpu.CompilerParams(
            dimension_semantics=("arbitrary",), vmem_limit_bytes=VMEM_LIMIT_BYTES),
        name="ffn",
    )(x2d, g, wg, wu, wd, *[w for w, _ in casts])
    return out[0], out[1:]


def _head_norm(x, gain):
    first = lax.broadcasted_iota(jnp.int32, (1, LANES), 1) < HEAD_DIM
    tiles = []
    for v in range(x.shape[1] // LANES):
        xt = x[:, v * LANES:(v + 1) * LANES]
        sq = xt * xt
        ss0 = jnp.sum(jnp.where(first, sq, 0.0), axis=-1, keepdims=True)
        ss1 = jnp.sum(jnp.where(first, 0.0, sq), axis=-1, keepdims=True)
        inv = jnp.where(first, lax.rsqrt(ss0 * (1.0 / HEAD_DIM) + EPS),
                        lax.rsqrt(ss1 * (1.0 / HEAD_DIM) + EPS))
        tiles.append(xt * inv)
    return jnp.concatenate(tiles, axis=1) * gain


def _proj_body(n_casts, x_ref, g_ref, w_ref, qg_ref, kg_ref, *rest):
    mix_ref, q_ref, k_ref, v_ref = rest[n_casts:n_casts + 4]
    _cast_blocks(rest[:n_casts], rest[n_casts + 4:])
    x = x_ref[...]
    h = (x * g_ref[...]).astype(BF16)
    u = jnp.dot(h, w_ref[...], preferred_element_type=F32) * _inv_rms(x)
    mix_ref[:, :CONV_W] = u[:, :CONV_W] * jax.nn.sigmoid(u[:, CONV_W:2 * CONV_W])
    mix_ref[:, CONV_W:] = u[:, 2 * CONV_W:MIX_IN_W]
    q = u[:, MIX_IN_W:MIX_IN_W + ATTN_W]
    k = u[:, MIX_IN_W + ATTN_W:MIX_IN_W + 2 * ATTN_W]
    v = u[:, MIX_IN_W + 2 * ATTN_W:]
    q_ref[...] = (_head_norm(q, qg_ref[...]) * (HEAD_DIM ** -0.5)).astype(BF16)
    k_ref[...] = _head_norm(k, kg_ref[...]).astype(BF16)
    v_ref[...] = v.astype(BF16)


def _proj(x2d, g, w_in, q_gain, k_gain, casts=()):
    t = x2d.shape[0]
    steps = t // PROJ_TOKENS
    plans = [_cast_plan(w, layer, steps) for w, layer in casts]
    tok = lambda w: pl.BlockSpec((PROJ_TOKENS, w), lambda i: (i, 0))
    out = pl.pallas_call(
        functools.partial(_proj_body, len(casts)),
        grid=(steps,),
        in_specs=[
            tok(D_MODEL),
            _const_spec((1, D_MODEL)),
            _const_spec((D_MODEL, IN_W)),
            _const_spec((1, ATTN_W)),
            _const_spec((1, ATTN_W)),
        ] + [p[0] for p in plans],
        out_specs=[tok(MIX_W), tok(ATTN_W), tok(ATTN_W), tok(ATTN_W)] + [p[1] for p in plans],
        out_shape=[
            jax.ShapeDtypeStruct((t, MIX_W), F32),
            jax.ShapeDtypeStruct((t, ATTN_W), BF16),
            jax.ShapeDtypeStruct((t, ATTN_W), BF16),
            jax.ShapeDtypeStruct((t, ATTN_W), BF16),
        ] + [p[2] for p in plans],
        compiler_params=pltpu.CompilerParams(
            dimension_semantics=("arbitrary",), vmem_limit_bytes=VMEM_LIMIT_BYTES),
        name="proj_in",
    )(x2d, g, w_in, q_gain, k_gain, *[w for w, _ in casts])
    return out[:4], out[4:]


def _bias_table_body(rpb_ref, o_ref):
    g = pl.program_id(0)
    qc = lax.broadcasted_iota(jnp.int32, (GRID_W, LANES), 0)
    lane = lax.broadcasted_iota(jnp.int32, (GRID_W, LANES), 1)
    kc = lane % GRID_W
    c0 = jnp.clip(qc - WIN_C // 2, 0, GRID_W - WIN_C)
    valid = (kc >= c0) & (kc < c0 + WIN_C)
    first = lane < GRID_W
    qc_all = lax.broadcasted_iota(jnp.int32, (N_DR * GRID_W, LANES), 0) % GRID_W
    for hh in range(HEAD_GROUP):
        head = g * HEAD_GROUP + hh
        t = jnp.concatenate(
            [jnp.broadcast_to(rpb_ref[head, d:d + 1, :], (GRID_W, LANES)) for d in range(N_DR)],
            axis=0)
        t = pltpu.roll(t, LANES - (WIN_C - 1), 1)
        for bit in range(GRID_W.bit_length() - 1):
            t = jnp.where(((qc_all >> bit) & 1) == 1, pltpu.roll(t, 1 << bit, 1), t)
        shifted = pltpu.roll(t, GRID_W, 1)
        for j in range(N_DR_PAIRS):
            pair = jnp.where(first, t[j * GRID_W:(j + 1) * GRID_W, :],
                             shifted[(j + 1) * GRID_W:(j + 2) * GRID_W, :])
            o_ref[0, j, hh * GRID_W:(hh + 1) * GRID_W, :] = jnp.where(valid, pair, NEG)


def _bias_table(rpb_l):
    padded = jnp.pad(rpb_l, ((0, 0), (0, 0), (0, LANES - rpb_l.shape[-1])))
    return pl.pallas_call(
        _bias_table_body,
        grid=(N_GROUPS,),
        in_specs=[_const_spec(padded.shape)],
        out_specs=pl.BlockSpec((1, N_DR_PAIRS, GROUP_W, 2 * GRID_W), lambda g: (g, 0, 0, 0)),
        out_shape=jax.ShapeDtypeStruct((N_GROUPS, N_DR_PAIRS, GROUP_W, 2 * GRID_W), F32),
        compiler_params=pltpu.CompilerParams(dimension_semantics=("arbitrary",)),
        name="bias_table",
    )(padded)


def _mix_body(x_ref, main_ref, left_ref, right_ref, q_ref, k_ref, v_ref, tab_ref,
              dw_ref, dwb_ref, lng_ref, lnb_ref, pw_ref, poolw_ref, pools_ref, wout_ref,
              o_ref, conv_ext, pool_ext, cat_ref):
    i = pl.program_id(1)
    n_blocks = pl.num_programs(1)
    seq = n_blocks * MIX_TOKENS

    left = left_ref[0]
    right = right_ref[0]
    lmask = jnp.where(i > 0, 1.0, 0.0).astype(F32)
    rmask = jnp.where(i < n_blocks - 1, 1.0, 0.0).astype(F32)
    main = main_ref[0]
    conv_ext[0:HALO, :] = left[:, :CONV_W] * lmask
    conv_ext[HALO:HALO + MIX_TOKENS, :] = main[:, :CONV_W]
    conv_ext[HALO + MIX_TOKENS:, :] = right[:, :CONV_W] * rmask
    pool_ext[0:HALO, :] = left[:, CONV_W:] * lmask
    pool_ext[HALO:HALO + MIX_TOKENS, :] = main[:, CONV_W:]
    pool_ext[HALO + MIX_TOKENS:, :] = right[:, CONV_W:] * rmask

    lane128 = lax.broadcasted_iota(jnp.int32, (ROW_CHUNK, 128), 1)
    first_half = lane128 < POOL_GROUP
    dwb = dwb_ref[...]
    lng = lng_ref[...]
    lnb = lnb_ref[...]
    pools = pools_ref[...]
    pw = pw_ref[...]
    poolw = poolw_ref[...]

    def conv_taps(c, residues, y):
        base = c * ROW_CHUNK
        for b in residues:
            part = None
            for a in range(-(HALO // SUBLANES), HALO // SUBLANES):
                j = SUBLANES * a + b + CONV_K // 2
                if 0 <= j < CONV_K:
                    off = HALO + base + SUBLANES * a
                    tap = jnp.concatenate(
                        [dw_ref[v, pl.ds(j, ROW_CHUNK + SUBLANES, stride=0), :]
                         for v in range(CONV_W // LANES)], axis=1)
                    term = conv_ext[off:off + ROW_CHUNK + SUBLANES, :] * tap
                    part = term if part is None else part + term
            y = y + part[b:b + ROW_CHUNK, :]
        return y

    def conv_finish(c, y):
        base = c * ROW_CHUNK
        mu = jnp.mean(y, axis=-1, keepdims=True)
        yc = y - mu
        var = jnp.mean(yc * yc, axis=-1, keepdims=True)
        yn = yc * lax.rsqrt(var + EPS) * lng + lnb
        swish = (yn * jax.nn.sigmoid(yn)).astype(BF16)
        c_out = jnp.dot(swish, pw, preferred_element_type=F32)
        cat_ref[base:base + ROW_CHUNK, 0:CONV_W] = c_out.astype(BF16)

    def pool_chunk(c):
        base = c * ROW_CHUNK

        margin = max(POOL_WINDOWS) // 2
        rows = ROW_CHUNK + 2 * margin
        lo_row = HALO + base - margin

        def after(x, d):
            return pltpu.roll(x, (-d) % rows, 0)

        def core(x):
            return x[margin:margin + ROW_CHUNK, :]

        e01 = pool_ext[lo_row:lo_row + rows, 0:LANES]
        e23 = pool_ext[lo_row:lo_row + rows, LANES:2 * LANES]
        pair01 = e01 + after(e01, 1)
        a2 = core(e01 + after(e01, -1))
        a4 = core(pair01 + after(pair01, -2))
        pair23 = e23 + after(e23, 1)
        quad23 = pair23 + after(pair23, 2)
        a8 = core(quad23 + after(quad23, -4))
        oct23 = quad23 + after(quad23, 4)
        a16 = oct23[0:ROW_CHUNK, :] + core(oct23)

        clipped = c == 0 or c == MIX_TOKENS // ROW_CHUNK - 1
        tpos = i * MIX_TOKENS + base + lax.broadcasted_iota(jnp.int32, (ROW_CHUNK, 128), 0)

        def mean(total, w):
            if not clipped:
                return total * (1.0 / w)
            lo = jnp.maximum(tpos - w // 2, 0)
            hi = jnp.minimum(tpos - w // 2 + w, seq)
            return total / (hi - lo).astype(F32)

        mean01 =jnp.where(first_half, mean(a2, 2), mean(a4, 4))
        mean23 = jnp.where(first_half, mean(a8, 8), mean(a16, 16))
        tok = pool_ext[HALO + base:HALO + base + ROW_CHUNK, :]
        mixed = (jnp.concatenate([mean01, mean23], axis=1) - tok).astype(BF16)
        p_out = jnp.dot(mixed, poolw, preferred_element_type=F32) * pools
        cat_ref[base:base + ROW_CHUNK, CONV_W:CONV_W + POOL_W] = p_out.astype(BF16)

    n_rows = n_blocks * MIX_ROWS
    lane_head = lax.broadcasted_iota(jnp.int32, (GRID_W, GROUP_W), 1) // HEAD_DIM

    def window(r):
        row = i * MIX_ROWS + r
        r0 = jnp.clip(row - WIN_R // 2, 0, n_rows - WIN_R)
        return pl.multiple_of(r0 * GRID_W, GRID_W), r0 - row + (WIN_R - 1)

    def att_scores(r, g):
        kstart, j0 = window(r)
        cols = slice(g * GROUP_W, (g + 1) * GROUP_W)
        q_r = q_ref[0, r * GRID_W:(r + 1) * GRID_W, cols]
        kwin = k_ref[0, pl.ds(kstart, WIN_R * GRID_W), cols]
        zero = jnp.zeros_like(q_r)
        q_bd = jnp.concatenate(
            [jnp.where(lane_head == hh, q_r, zero) for hh in range(HEAD_GROUP)], axis=0)
        return lax.dot_general(q_bd, kwin, (((1,), (1,)), ((), ())),
                               preferred_element_type=F32)

    def att_softmax(r, g, sc):
        _, j0 = window(r)
        bias = jnp.concatenate(
            [tab_ref[g, j0 + 2 * m] for m in range(WIN_R // 2)], axis=1)
        sc = sc + bias
        mx = jnp.max(sc, axis=-1, keepdims=True)
        p = jnp.exp(sc - mx)
        return p.astype(BF16), jnp.sum(p, axis=-1, keepdims=True)

    def att_values(r, g, p, den):
        kstart, _ = window(r)
        cols = slice(g * GROUP_W, (g + 1) * GROUP_W)
        vwin = v_ref[0, pl.ds(kstart, WIN_R * GRID_W), cols]
        return jnp.dot(p, vwin, preferred_element_type=F32) / den

    def att_finish(r, g, o_all):
        out = jnp.zeros((GRID_W, GROUP_W), F32)
        for hh in range(HEAD_GROUP):
            out = jnp.where(lane_head == hh, o_all[hh * GRID_W:(hh + 1) * GRID_W, :], out)
        cat_ref[r * GRID_W:(r + 1) * GRID_W,
                CONV_W + POOL_W + g * GROUP_W:CONV_W + POOL_W + (g + 1) * GROUP_W] = (
                    out.astype(BF16))

    for c in range(MIX_TOKENS // ROW_CHUNK):
        conv_finish(c, conv_taps(c, range(SUBLANES), dwb))
        pool_chunk(c)
    for r in range(MIX_ROWS):
        for g in range(N_GROUPS):
            p, den = att_softmax(r, g, att_scores(r, g))
            att_finish(r, g, att_values(r, g, p, den))

    o_ref[0] = x_ref[0] + jnp.dot(cat_ref[...], wout_ref[...], preferred_element_type=F32)


def _mix(x3d, mix3d, q3d, k3d, v3d, table, dw, dwb, lng, lnb, pw, poolw, pools, wout):
    b, s, _ = x3d.shape
    n_blocks = s // MIX_TOKENS
    per_blk = MIX_TOKENS // HALO
    n_halo = s // HALO
    blk = lambda w: pl.BlockSpec((1, MIX_TOKENS, w), lambda bi, i: (bi, i, 0))
    full = lambda w: pl.BlockSpec((1, s, w), lambda bi, i: (bi, 0, 0))
    return pl.pallas_call(
        _mix_body,
        grid=(b, n_blocks),
        in_specs=[
            blk(D_MODEL),
            blk(MIX_W),
            pl.BlockSpec((1, HALO, MIX_W),
                         lambda bi, i: (bi, jnp.maximum(i * per_blk - 1, 0), 0)),
            pl.BlockSpec((1, HALO, MIX_W),
                         lambda bi, i: (bi, jnp.minimum((i + 1) * per_blk, n_halo - 1), 0)),
            blk(ATTN_W),
            full(ATTN_W),
            full(ATTN_W),
            _const_spec(table.shape),
            _const_spec(dw.shape),
            _const_spec(dwb.shape),
            _const_spec(lng.shape),
            _const_spec(lnb.shape),
            _const_spec(pw.shape),
            _const_spec(poolw.shape),
            _const_spec(pools.shape),
            _const_spec(wout.shape),
        ],
        out_specs=blk(D_MODEL),
        out_shape=jax.ShapeDtypeStruct((b, s, D_MODEL), F32),
        scratch_shapes=[
            pltpu.VMEM((MIX_TOKENS + 2 * HALO, CONV_W), F32),
            pltpu.VMEM((MIX_TOKENS + 2 * HALO, POOL_W), F32),
            pltpu.VMEM((MIX_TOKENS, D_MODEL), BF16),
        ],
        compiler_params=pltpu.CompilerParams(
            dimension_semantics=("arbitrary", "arbitrary"), vmem_limit_bytes=VMEM_LIMIT_BYTES),
        name="mixer",
    )(x3d, mix3d, mix3d, mix3d, q3d, k3d, v3d, table, dw, dwb, lng, lnb, pw, poolw, pools, wout)


def _block_diag(blocks):
    n, r, c = blocks.shape
    eye = jnp.eye(n, dtype=blocks.dtype)
    return (eye[:, None, :, None] * blocks[:, :, None, :]).reshape(n * r, n * c)


def kernel(x, ffn1_norm, ffn1_gate, ffn1_up, ffn1_down, mix_norm, w_in, conv_dw, conv_dw_b,
           conv_ln_g, conv_ln_b, conv_pw, pool_w, pool_scale, q_norm, k_norm, rpb, w_out,
           ffn2_norm, ffn2_gate, ffn2_up, ffn2_down):
    b, s, d = x.shape
    assert d == D_MODEL and s % MIX_TOKENS == 0 and (b * s) % FFN_TOKENS == 0
    t = b * s
    row = lambda a: a.reshape(1, -1)

    x2d = x.reshape(t, d)
    ffn1_w = [w[0].astype(BF16) for w in (ffn1_gate, ffn1_up, ffn1_down)]
    for l in range(DEPTH):
        x2d, (gate2, up2, down2, w_in_l) = _ffn(
            x2d, row(ffn1_norm[l]), *ffn1_w,
            casts=((ffn2_gate, l), (ffn2_up, l), (ffn2_down, l), (w_in, l)))
        (mix_in, q, k, v), (w_out_l, conv_pw_l) = _proj(
            x2d, row(mix_norm[l]), w_in_l,
            row(jnp.tile(q_norm[l], N_HEADS)), row(jnp.tile(k_norm[l], N_HEADS)),
            casts=((w_out, l), (conv_pw, l)))
        table = _bias_table(rpb[l])
        x3d = _mix(
            x2d.reshape(b, s, d), mix_in.reshape(b, s, MIX_W),
            q.reshape(b, s, ATTN_W), k.reshape(b, s, ATTN_W), v.reshape(b, s, ATTN_W),
            table, conv_dw[l].reshape(CONV_K, CONV_W // LANES, LANES).transpose(1, 0, 2),
            row(conv_dw_b[l]), row(conv_ln_g[l]), row(conv_ln_b[l]),
            conv_pw_l, _block_diag(pool_w[l]).astype(BF16),
            row(pool_scale[l]), w_out_l)
        next_ffn1 = (() if l + 1 == DEPTH else
                     ((ffn1_gate, l + 1), (ffn1_up, l + 1), (ffn1_down, l + 1)))
        x2d, ffn1_w = _ffn(x3d.reshape(t, d), row(ffn2_norm[l]), gate2, up2, down2,
                           casts=next_ffn1)
    return x2d.reshape(b, s, d)
```

```python
import functools

import jax
import jax.numpy as jnp
from jax import lax
from jax.experimental import pallas as pl
from jax.experimental.pallas import tpu as pltpu

D_MODEL = 1024
DEPTH = 2
CONV_W = 256
POOL_W = 256
ATTN_W = 512
HEAD_DIM = 64
N_HEADS = 8
CONV_K = 31
POOL_WINDOWS = (2, 4, 8, 16)
POOL_GROUP = 64
GRID_W = 64
WIN_R = 8
WIN_C = 16
D_FF = 2816
IN_W = 2 * CONV_W + POOL_W + 3 * ATTN_W
MIX_IN_W = 2 * CONV_W + POOL_W
MIX_W = CONV_W + POOL_W
EPS = 1e-6
NEG = -1e30

F32 = jnp.float32
BF16 = jnp.bfloat16

VMEM_LIMIT_BYTES = 56 * 1024 * 1024

LANES = 128
SUBLANES = 8
BF16_SUBLANES = 16
FFN_TOKENS = 1024
MXU_TILE = 256
FF_CHUNKS = (6 * MXU_TILE, 5 * MXU_TILE)
PROJ_TOKENS = 1024
MIX_ROWS = 8
MIX_TOKENS = MIX_ROWS * GRID_W
HALO = 16
ROW_CHUNK = 2 * GRID_W
assert sum(FF_CHUNKS) == D_FF and ROW_CHUNK >= max(POOL_WINDOWS) // 2
HEAD_GROUP = 4
GROUP_W = HEAD_GROUP * HEAD_DIM
N_GROUPS = N_HEADS // HEAD_GROUP
N_DR = 2 * WIN_R - 1
N_DR_PAIRS = N_DR - 1


def _const_spec(shape):
    nd = len(shape)
    return pl.BlockSpec(shape, lambda *_: (0,) * nd, pipeline_mode=pl.Buffered(1))


def _inv_rms(x):
    return lax.rsqrt(jnp.mean(x * x, axis=-1, keepdims=True) + EPS)


def _ffn_compute(x, g_ref, wg_ref, wu_ref, wd_ref):
    inv = _inv_rms(x)
    h = (x * g_ref[...]).astype(BF16)
    y = None
    start = 0
    for width in FF_CHUNKS:
        sl = slice(start, start + width)
        start += width
        gate = jnp.dot(h, wg_ref[:, sl], preferred_element_type=F32) * inv
        up = jnp.dot(h, wu_ref[:, sl], preferred_element_type=F32) * inv
        act = (gate * jax.nn.sigmoid(gate) * up).astype(BF16)
        part = jnp.dot(act, wd_ref[sl, :], preferred_element_type=F32)
        y = part if y is None else y + part
    return x + 0.5 * y


def _cast_plan(stacked, layer, steps):
    _, rows, cols = stacked.shape
    tiles = rows // BF16_SUBLANES
    assert tiles * BF16_SUBLANES == rows
    n_blk = max(d for d in range(1, min(steps, tiles) + 1) if tiles % d == 0)
    blk_rows = rows // n_blk
    block_of = lambda i: (i * n_blk) // steps
    return (pl.BlockSpec((1, blk_rows, cols), lambda i: (layer, block_of(i), 0)),
            pl.BlockSpec((blk_rows, cols), lambda i: (block_of(i), 0)),
            jax.ShapeDtypeStruct((rows, cols), BF16))


def _cast_blocks(src_refs, dst_refs):
    for src, dst in zip(src_refs, dst_refs):
        dst[...] = src[0].astype(BF16)


def _ffn_body(n_casts, x_ref, g_ref, wg_ref, wu_ref, wd_ref, *rest):
    o_ref = rest[n_casts]
    o_ref[...] = _ffn_compute(x_ref[...], g_ref, wg_ref, wu_ref, wd_ref)
    _cast_blocks(rest[:n_casts], rest[n_casts + 1:])


def _ffn(x2d, g, wg, wu, wd, casts=()):
    t = x2d.shape[0]
    steps = t // FFN_TOKENS
    plans = [_cast_plan(w, layer, steps) for w, layer in casts]
    out = pl.pallas_call(
        functools.partial(_ffn_body, len(casts)),
        grid=(steps,),
        in_specs=[
            pl.BlockSpec((FFN_TOKENS, D_MODEL), lambda i: (i, 0)),
            _const_spec((1, D_MODEL)),
            _const_spec((D_MODEL, D_FF)),
            _const_spec((D_MODEL, D_FF)),
            _const_spec((D_FF, D_MODEL)),
        ] + [p[0] for p in plans],
        out_specs=[pl.BlockSpec((FFN_TOKENS, D_MODEL), lambda i: (i, 0))] + [p[1] for p in plans],
        out_shape=[jax.ShapeDtypeStruct((t, D_MODEL), F32)] + [p[2] for p in plans],
        compiler_params=pltpu.CompilerParams(
            dimension_semantics=("arbitrary",), vmem_limit_bytes=VMEM_LIMIT_BYTES),
        name="ffn",
    )(x2d, g, wg, wu, wd, *[w for w, _ in casts])
    return out[0], out[1:]


def _head_norm(x, gain):
    first = lax.broadcasted_iota(jnp.int32, (1, LANES), 1) < HEAD_DIM
    tiles = []
    for v in range(x.shape[1] // LANES):
        xt = x[:, v * LANES:(v + 1) * LANES]
        sq = xt * xt
        ss0 = jnp.sum(jnp.where(first, sq, 0.0), axis=-1, keepdims=True)
        ss1 = jnp.sum(jnp.where(first, 0.0, sq), axis=-1, keepdims=True)
        inv = jnp.where(first, lax.rsqrt(ss0 * (1.0 / HEAD_DIM) + EPS),
                        lax.rsqrt(ss1 * (1.0 / HEAD_DIM) + EPS))
        tiles.append(xt * inv)
    return jnp.concatenate(tiles, axis=1) * gain


def _proj_body(n_casts, x_ref, g_ref, w_ref, qg_ref, kg_ref, *rest):
    mix_ref, q_ref, k_ref, v_ref = rest[n_casts:n_casts + 4]
    _cast_blocks(rest[:n_casts], rest[n_casts + 4:])
    x = x_ref[...]
    h = (x * g_ref[...]).astype(BF16)
    u = jnp.dot(h, w_ref[...], preferred_element_type=F32) * _inv_rms(x)
    mix_ref[:, :CONV_W] = u[:, :CONV_W] * jax.nn.sigmoid(u[:, CONV_W:2 * CONV_W])
    mix_ref[:, CONV_W:] = u[:, 2 * CONV_W:MIX_IN_W]
    q = u[:, MIX_IN_W:MIX_IN_W + ATTN_W]
    k = u[:, MIX_IN_W + ATTN_W:MIX_IN_W + 2 * ATTN_W]
    v = u[:, MIX_IN_W + 2 * ATTN_W:]
    q_ref[...] = (_head_norm(q, qg_ref[...]) * (HEAD_DIM ** -0.5)).astype(BF16)
    k_ref[...] = _head_norm(k, kg_ref[...]).astype(BF16)
    v_ref[...] = v.astype(BF16)


def _proj(x2d, g, w_in, q_gain, k_gain, casts=()):
    t = x2d.shape[0]
    steps = t // PROJ_TOKENS
    plans = [_cast_plan(w, layer, steps) for w, layer in casts]
    tok = lambda w: pl.BlockSpec((PROJ_TOKENS, w), lambda i: (i, 0))
    out = pl.pallas_call(
        functools.partial(_proj_body, len(casts)),
        grid=(steps,),
        in_specs=[
            tok(D_MODEL),
            _const_spec((1, D_MODEL)),
            _const_spec((D_MODEL, IN_W)),
            _const_spec((1, ATTN_W)),
            _const_spec((1, ATTN_W)),
        ] + [p[0] for p in plans],
        out_specs=[tok(MIX_W), tok(ATTN_W), tok(ATTN_W), tok(ATTN_W)] + [p[1] for p in plans],
        out_shape=[
            jax.ShapeDtypeStruct((t, MIX_W), F32),
            jax.ShapeDtypeStruct((t, ATTN_W), BF16),
            jax.ShapeDtypeStruct((t, ATTN_W), BF16),
            jax.ShapeDtypeStruct((t, ATTN_W), BF16),
        ] + [p[2] for p in plans],
        compiler_params=pltpu.CompilerParams(
            dimension_semantics=("arbitrary",), vmem_limit_bytes=VMEM_LIMIT_BYTES),
        name="proj_in",
    )(x2d, g, w_in, q_gain, k_gain, *[w for w, _ in casts])
    return out[:4], out[4:]


def _bias_table_rows(rpb_ref, o_ref, layer, head):
    qc = lax.broadcasted_iota(jnp.int32, (GRID_W, LANES), 0)
    lane = lax.broadcasted_iota(jnp.int32, (GRID_W, LANES), 1)
    kc = lane % GRID_W
    c0 = jnp.clip(qc - WIN_C // 2, 0, GRID_W - WIN_C)
    valid = (kc >= c0) & (kc < c0 + WIN_C)
    first = lane < GRID_W
    qc_all = lax.broadcasted_iota(jnp.int32, (N_DR * GRID_W, LANES), 0) % GRID_W
    t = jnp.concatenate(
        [jnp.broadcast_to(rpb_ref[layer, head, d:d + 1, :], (GRID_W, LANES))
         for d in range(N_DR)], axis=0)
    t = pltpu.roll(t, LANES - (WIN_C - 1), 1)
    for bit in range(GRID_W.bit_length() - 1):
        t = jnp.where(((qc_all >> bit) & 1) == 1, pltpu.roll(t, 1 << bit, 1), t)
    shifted = pltpu.roll(t, GRID_W, 1)
    rows = pl.ds(pl.multiple_of(lax.rem(head, HEAD_GROUP) * GRID_W, GRID_W), GRID_W)
    for j in range(N_DR_PAIRS):
        pair = jnp.where(first, t[j * GRID_W:(j + 1) * GRID_W, :],
                         shifted[(j + 1) * GRID_W:(j + 2) * GRID_W, :])
        o_ref[0, 0, j, rows, :] = jnp.where(valid, pair, NEG)


def _prep_body(rpb_ref, *rest):
    n_casts = (len(rest) - 1) // 2
    tab_ref = rest[n_casts]
    _cast_blocks(rest[:n_casts], rest[n_casts + 1:])
    step = pl.program_id(0)
    _bias_table_rows(rpb_ref, tab_ref, step // N_HEADS, lax.rem(step, N_HEADS))


def _prep(rpb, casts):
    depth = rpb.shape[0]
    steps = depth * N_HEADS
    padded = jnp.pad(rpb, ((0, 0), (0, 0), (0, 0), (0, LANES - rpb.shape[-1])))
    plans = [_cast_plan(w, layer, steps) for w, layer in casts]

    def table_block(step):
        return step // N_HEADS, (step % N_HEADS) // HEAD_GROUP, 0, 0, 0

    out = pl.pallas_call(
        _prep_body,
        grid=(steps,),
        in_specs=[_const_spec(padded.shape)] + [p[0] for p in plans],
        out_specs=[pl.BlockSpec((1, 1, N_DR_PAIRS, GROUP_W, 2 * GRID_W), table_block)]
        + [p[1] for p in plans],
        out_shape=[jax.ShapeDtypeStruct((depth, N_GROUPS, N_DR_PAIRS, GROUP_W, 2 * GRID_W), F32)]
        + [p[2] for p in plans],
        compiler_params=pltpu.CompilerParams(
            dimension_semantics=("arbitrary",), vmem_limit_bytes=VMEM_LIMIT_BYTES),
        name="prep",
    )(padded, *[w for w, _ in casts])
    return out[0], out[1:]


def _mix_body(x_ref, main_ref, left_ref, right_ref, q_ref, k_ref, v_ref, tab_ref,
              dw_ref, dwb_ref, lng_ref, lnb_ref, pw_ref, poolw_ref, pools_ref, wout_ref,
              o_ref, conv_ext, pool_ext, cat_ref):
    i = pl.program_id(1)
    n_blocks = pl.num_programs(1)
    seq = n_blocks * MIX_TOKENS

    left = left_ref[0]
    right = right_ref[0]
    lmask = jnp.where(i > 0, 1.0, 0.0).astype(F32)
    rmask = jnp.where(i < n_blocks - 1, 1.0, 0.0).astype(F32)
    main = main_ref[0]
    conv_ext[0:HALO, :] = left[:, :CONV_W] * lmask
    conv_ext[HALO:HALO + MIX_TOKENS, :] = main[:, :CONV_W]
    conv_ext[HALO + MIX_TOKENS:, :] = right[:, :CONV_W] * rmask
    pool_ext[0:HALO, :] = left[:, CONV_W:] * lmask
    pool_ext[HALO:HALO + MIX_TOKENS, :] = main[:, CONV_W:]
    pool_ext[HALO + MIX_TOKENS:, :] = right[:, CONV_W:] * rmask

    lane128 = lax.broadcasted_iota(jnp.int32, (ROW_CHUNK, 128), 1)
    first_half = lane128 < POOL_GROUP
    dwb = dwb_ref[...]
    lng = lng_ref[...]
    lnb = lnb_ref[...]
    pools = pools_ref[...]
    pw = pw_ref[...]
    poolw = poolw_ref[...]

    def conv_taps(c, residues, y):
        base = c * ROW_CHUNK
        for b in residues:
            part = None
            for a in range(-(HALO // SUBLANES), HALO // SUBLANES):
                j = SUBLANES * a + b + CONV_K // 2
                if 0 <= j < CONV_K:
                    off = HALO + base + SUBLANES * a
                    tap = jnp.concatenate(
                        [dw_ref[v, pl.ds(j, ROW_CHUNK + SUBLANES, stride=0), :]
                         for v in range(CONV_W // LANES)], axis=1)
                    term = conv_ext[off:off + ROW_CHUNK + SUBLANES, :] * tap
                    part = term if part is None else part + term
            y = y + part[b:b + ROW_CHUNK, :]
        return y

    def conv_finish(c, y):
        base = c * ROW_CHUNK
        mu = jnp.mean(y, axis=-1, keepdims=True)
        yc = y - mu
        var = jnp.mean(yc * yc, axis=-1, keepdims=True)
        yn = yc * lax.rsqrt(var + EPS) * lng + lnb
        swish = (yn * jax.nn.sigmoid(yn)).astype(BF16)
        c_out = jnp.dot(swish, pw, preferred_element_type=F32)
        cat_ref[base:base + ROW_CHUNK, 0:CONV_W] = c_out.astype(BF16)

    def pool_chunk(c):
        base = c * ROW_CHUNK

        margin = max(POOL_WINDOWS) // 2
        rows = ROW_CHUNK + 2 * margin
        lo_row = HALO + base - margin

        def after(x, d):
            return pltpu.roll(x, (-d) % rows, 0)

        def core(x):
            return x[margin:margin + ROW_CHUNK, :]

        e01 = pool_ext[lo_row:lo_row + rows, 0:LANES]
        e23 = pool_ext[lo_row:lo_row + rows, LANES:2 * LANES]
        pair01 = e01 + after(e01, 1)
        a2 = core(e01 + after(e01, -1))
        a4 = core(pair01 + after(pair01, -2))
        pair23 = e23 + after(e23, 1)
        quad23 = pair23 + after(pair23, 2)
        a8 = core(quad23 + after(quad23, -4))
        oct23 = quad23 + after(quad23, 4)
        a16 = oct23[0:ROW_CHUNK, :] + core(oct23)

        clipped = c == 0 or c == MIX_TOKENS // ROW_CHUNK - 1
        tpos = i * MIX_TOKENS + base + lax.broadcasted_iota(jnp.int32, (ROW_CHUNK, 128), 0)

        def mean(total, w):
            if not clipped:
                return total * (1.0 / w)
            lo = jnp.maximum(tpos - w // 2, 0)
            hi = jnp.minimum(tpos - w // 2 + w, seq)
            return total / (hi - lo).astype(F32)

        mean01 =jnp.where(first_half, mean(a2, 2), mean(a4, 4))
        mean23 = jnp.where(first_half, mean(a8, 8), mean(a16, 16))
        tok = pool_ext[HALO + base:HALO + base + ROW_CHUNK, :]
        mixed = (jnp.concatenate([mean01, mean23], axis=1) - tok).astype(BF16)
        p_out = jnp.dot(mixed, poolw, preferred_element_type=F32) * pools
        cat_ref[base:base + ROW_CHUNK, CONV_W:CONV_W + POOL_W] = p_out.astype(BF16)

    n_rows = n_blocks * MIX_ROWS
    lane_head = lax.broadcasted_iota(jnp.int32, (GRID_W, GROUP_W), 1) // HEAD_DIM

    def window(r):
        row = i * MIX_ROWS + r
        r0 = jnp.clip(row - WIN_R // 2, 0, n_rows - WIN_R)
        return pl.multiple_of(r0 * GRID_W, GRID_W), r0 - row + (WIN_R - 1)

    def att_scores(r, g):
        kstart, j0 = window(r)
        cols = slice(g * GROUP_W, (g + 1) * GROUP_W)
        q_r = q_ref[0, r * GRID_W:(r + 1) * GRID_W, cols]
        kwin = k_ref[0, pl.ds(kstart, WIN_R * GRID_W), cols]
        zero = jnp.zeros_like(q_r)
        q_bd = jnp.concatenate(
            [jnp.where(lane_head == hh, q_r, zero) for hh in range(HEAD_GROUP)], axis=0)
        return lax.dot_general(q_bd, kwin, (((1,), (1,)), ((), ())),
                               preferred_element_type=F32)

    def att_softmax(r, g, sc):
        _, j0 = window(r)
        bias = jnp.concatenate(
            [tab_ref[g, j0 + 2 * m] for m in range(WIN_R // 2)], axis=1)
        sc = sc + bias
        mx = jnp.max(sc, axis=-1, keepdims=True)
        p = jnp.exp(sc - mx)
        return p.astype(BF16), jnp.sum(p, axis=-1, keepdims=True)

    def att_values(r, g, p, den):
        kstart, _ = window(r)
        cols = slice(g * GROUP_W, (g + 1) * GROUP_W)
        vwin = v_ref[0, pl.ds(kstart, WIN_R * GRID_W), cols]
        return jnp.dot(p, vwin, preferred_element_type=F32) / den

    def att_finish(r, g, o_all):
        out = jnp.zeros((GRID_W, GROUP_W), F32)
        for hh in range(HEAD_GROUP):
            out = jnp.where(lane_head == hh, o_all[hh * GRID_W:(hh + 1) * GRID_W, :], out)
        cat_ref[r * GRID_W:(r + 1) * GRID_W,
                CONV_W + POOL_W + g * GROUP_W:CONV_W + POOL_W + (g + 1) * GROUP_W] = (
                    out.astype(BF16))

    for c in range(MIX_TOKENS // ROW_CHUNK):
        conv_finish(c, conv_taps(c, range(SUBLANES), dwb))
        pool_chunk(c)
    for r in range(MIX_ROWS):
        for g in range(N_GROUPS):
            p, den = att_softmax(r, g, att_scores(r, g))
            att_finish(r, g, att_values(r, g, p, den))

    o_ref[0] = x_ref[0] + jnp.dot(cat_ref[...], wout_ref[...], preferred_element_type=F32)


def _mix(x3d, mix3d, q3d, k3d, v3d, tables, layer, dw, dwb, lng, lnb, pw, poolw, pools, wout):
    b, s, _ = x3d.shape
    n_blocks = s // MIX_TOKENS
    per_blk = MIX_TOKENS // HALO
    n_halo = s // HALO
    blk = lambda w: pl.BlockSpec((1, MIX_TOKENS, w), lambda bi, i: (bi, i, 0))
    full = lambda w: pl.BlockSpec((1, s, w), lambda bi, i: (bi, 0, 0))
    return pl.pallas_call(
        _mix_body,
        grid=(b, n_blocks),
        in_specs=[
            blk(D_MODEL),
            blk(MIX_W),
            pl.BlockSpec((1, HALO, MIX_W),
                         lambda bi, i: (bi, jnp.maximum(i * per_blk - 1, 0), 0)),
            pl.BlockSpec((1, HALO, MIX_W),
                         lambda bi, i: (bi, jnp.minimum((i + 1) * per_blk, n_halo - 1), 0)),
            blk(ATTN_W),
            full(ATTN_W),
            full(ATTN_W),
            pl.BlockSpec((None,) + tables.shape[1:], lambda bi, i: (layer, 0, 0, 0, 0),
                         pipeline_mode=pl.Buffered(1)),
            _const_spec(dw.shape),
            _const_spec(dwb.shape),
            _const_spec(lng.shape),
            _const_spec(lnb.shape),
            _const_spec(pw.shape),
            _const_spec(poolw.shape),
            _const_spec(pools.shape),
            _const_spec(wout.shape),
        ],
        out_specs=blk(D_MODEL),
        out_shape=jax.ShapeDtypeStruct((b, s, D_MODEL), F32),
        scratch_shapes=[
            pltpu.VMEM((MIX_TOKENS + 2 * HALO, CONV_W), F32),
            pltpu.VMEM((MIX_TOKENS + 2 * HALO, POOL_W), F32),
            pltpu.VMEM((MIX_TOKENS, D_MODEL), BF16),
        ],
        compiler_params=pltpu.CompilerParams(
            dimension_semantics=("arbitrary", "arbitrary"), vmem_limit_bytes=VMEM_LIMIT_BYTES),
        name="mixer",
    )(x3d, mix3d, mix3d, mix3d, q3d, k3d, v3d, tables, dw, dwb, lng, lnb, pw, poolw, pools, wout)


def _block_diag(blocks):
    n, r, c = blocks.shape
    eye = jnp.eye(n, dtype=blocks.dtype)
    return (eye[:, None, :, None] * blocks[:, :, None, :]).reshape(n * r, n * c)


def kernel(x, ffn1_norm, ffn1_gate, ffn1_up, ffn1_down, mix_norm, w_in, conv_dw, conv_dw_b,
           conv_ln_g, conv_ln_b, conv_pw, pool_w, pool_scale, q_norm, k_norm, rpb, w_out,
           ffn2_norm, ffn2_gate, ffn2_up, ffn2_down):
    b, s, d = x.shape
    assert d == D_MODEL and s % MIX_TOKENS == 0 and (b * s) % FFN_TOKENS == 0
    t = b * s
    row = lambda a: a.reshape(1, -1)

    x2d = x.reshape(t, d)
    tables, ffn1_w = _prep(rpb, ((ffn1_gate, 0), (ffn1_up, 0), (ffn1_down, 0)))
    for l in range(DEPTH):
        x2d, (gate2, up2, down2, w_in_l) = _ffn(
            x2d, row(ffn1_norm[l]), *ffn1_w,
            casts=((ffn2_gate, l), (ffn2_up, l), (ffn2_down, l), (w_in, l)))
        (mix_in, q, k, v), (w_out_l, conv_pw_l) = _proj(
            x2d, row(mix_norm[l]), w_in_l,
            row(jnp.tile(q_norm[l], N_HEADS)), row(jnp.tile(k_norm[l], N_HEADS)),
            casts=((w_out, l), (conv_pw, l)))
        x3d = _mix(
            x2d.reshape(b, s, d), mix_in.reshape(b, s, MIX_W),
            q.reshape(b, s, ATTN_W), k.reshape(b, s, ATTN_W), v.reshape(b, s, ATTN_W),
            tables, l, conv_dw[l].reshape(CONV_K, CONV_W // LANES, LANES).transpose(1, 0, 2),
            row(conv_dw_b[l]), row(conv_ln_g[l]), row(conv_ln_b[l]),
            conv_pw_l, _block_diag(pool_w[l]).astype(BF16),
            row(pool_scale[l]), w_out_l)
        next_ffn1 = (() if l + 1 == DEPTH else
                     ((ffn1_gate, l + 1), (ffn1_up, l + 1), (ffn1_down, l + 1)))
        x2d, ffn1_w = _ffn(x3d.reshape(t, d), row(ffn2_norm[l]), gate2, up2, down2,
                           casts=next_ffn1)
    return x2d.reshape(b, s, d)
```
